```python
import math
import jax, jax.numpy as jnp
from jax import lax
import numpy as np

D_MODEL = 2048
BATCH = 1
SEQ = 16384
DEPTH = 1
DEC_BATCH = 16
DEC_SEQ = 16
PAST_LEN = 1024

CHUNK = 64
Q_BLOCK = 128
N_HEADS_A = 16
HEAD_DIM_A = 128
IDX_HEADS = 16
IDX_DIM = 64
TOPK_MAX = 256
REL_BUCKETS = 32
REL_MAX_DIST = 128
N_HEADS_B = 16
QK_NOPE_DIM = 128
ROPE_DIM = 64
V_DIM_B = 128
Q_LORA = 512
KV_LORA = 256
ROPE_THETA = 10000.0
MLA_SCALE = (QK_NOPE_DIM + ROPE_DIM) ** -0.5
A_SCALE = HEAD_DIM_A ** -0.5
D_FF = 4 * D_MODEL
EPS = 1e-6
NEG_INF = -1e30
COL_SIZES = (N_HEADS_A * HEAD_DIM_A, N_HEADS_A * HEAD_DIM_A, N_HEADS_A * HEAD_DIM_A,
             IDX_HEADS * IDX_DIM, IDX_DIM, IDX_HEADS,
             Q_LORA, KV_LORA, ROPE_DIM,
             D_MODEL, D_MODEL)
IN_COLS = sum(COL_SIZES)

kernel_name = "hybrid_dsa_mla_streaming_step"


def rmsnorm(x, g):
    xf = x.astype(jnp.float32)
    y = xf * lax.rsqrt(jnp.mean(xf * xf, axis=-1, keepdims=True) + EPS)
    return (y * g.astype(jnp.float32)).astype(x.dtype)


def split_cols(z):
    outs, off = [], 0
    for n in COL_SIZES:
        outs.append(z[..., off:off + n])
        off += n
    return outs


def rope(x, pos):
    half = ROPE_DIM // 2
    inv_freq = jnp.power(ROPE_THETA, -jnp.arange(half, dtype=jnp.float32) / half)
    ang = pos.astype(jnp.float32)[:, None] * inv_freq[None, :]
    shp = (ang.shape[0],) + (1,) * (x.ndim - 3) + (half,)
    cos = jnp.cos(ang).reshape(shp)
    sin = jnp.sin(ang).reshape(shp)
    x1 = x[..., :half].astype(jnp.float32)
    x2 = x[..., half:].astype(jnp.float32)
    return jnp.concatenate([x1 * cos - x2 * sin, x1 * sin + x2 * cos], axis=-1).astype(x.dtype)


def t5_bucket(rel):
    nb = REL_BUCKETS // 2
    ret = (rel > 0).astype(jnp.int32) * nb
    n = jnp.abs(rel)
    max_exact = nb // 2
    large = max_exact + (jnp.log(jnp.maximum(n, 1).astype(jnp.float32) / max_exact)
                         / math.log(REL_MAX_DIST / max_exact) * (nb - max_exact)).astype(jnp.int32)
    large = jnp.minimum(large, nb - 1)
    return ret + jnp.where(n < max_exact, n, large)


def dsa_attend(q, ix_q, ix_w, qpos, k_all, v_all, ixk_all, kpos, rel_table, topk):
    dots = jnp.einsum('bthd,bsd->bths', ix_q, ixk_all)
    score = jnp.einsum('bths,bth->bts', jax.nn.relu(dots), ix_w).astype(jnp.float32)
    admissible = (kpos[None, :] // CHUNK) <= (qpos[:, None] // CHUNK)
    score = jnp.where(admissible[None], score, NEG_INF)
    top_val, top_idx = lax.top_k(score, topk)
    valid = top_val > 0.5 * NEG_INF
    take = jax.vmap(lambda rows, idx: rows[idx])
    k_sel = take(k_all, top_idx)
    v_sel = take(v_all, top_idx)
    logits = jnp.einsum('bthd,btkhd->bhtk', q, k_sel).astype(jnp.float32) * A_SCALE
    rel = kpos[top_idx] - qpos[None, :, None]
    bias = jnp.moveaxis(rel_table[t5_bucket(rel)], -1, 1).astype(jnp.float32)
    logits = jnp.where(valid[:, None], logits + bias, NEG_INF)
    p = jax.nn.softmax(logits, axis=-1).astype(v_sel.dtype)
    out = jnp.einsum('bhtk,btkhd->bthd', p, v_sel)
    return out.reshape(out.shape[0], out.shape[1], -1)


def mla_attend(q_nope, q_rope, qpos, k_nope, k_rope, v, kpos):
    logits = (jnp.einsum('bthd,bshd->bhts', q_nope, k_nope)
              + jnp.einsum('bthr,bsr->bhts', q_rope, k_rope)).astype(jnp.float32) * MLA_SCALE
    visible = (kpos[None, :] // CHUNK) <= (qpos[:, None] // CHUNK)
    logits = jnp.where(visible[None, None], logits, NEG_INF)
    p = jax.nn.softmax(logits, axis=-1).astype(v.dtype)
    out = jnp.einsum('bhts,bshd->bthd', p, v)
    return out.reshape(out.shape[0], out.shape[1], -1)


def to_blocks(a):
    b, l = a.shape[0], a.shape[1]
    return a.reshape((b, l // Q_BLOCK, Q_BLOCK) + a.shape[2:]).swapaxes(0, 1)


def from_blocks(a):
    nb, b, q = a.shape[0], a.shape[1], a.shape[2]
    return a.swapaxes(0, 1).reshape((b, nb * q) + a.shape[3:])


def trunk_layer(x, past, rel_table, norm_mix_g, w_in, q_lora_g, w_uq, kv_lora_g, w_uk, w_uv,
                w_out, norm_ffn_g, w_ff_up, w_ff_down, topk, blocked):
    b, l, _ = x.shape
    p_len = 0 if past is None else past[0].shape[1]
    qpos = p_len + jnp.arange(l, dtype=jnp.int32)
    kpos = jnp.arange(p_len + l, dtype=jnp.int32)

    h = rmsnorm(x, norm_mix_g)
    z = h @ w_in
    a_q, a_k, a_v, ix_q, ix_k, ix_w, cq, ckv, kr, gate_a, gate_b = split_cols(z)
    a_q = a_q.reshape(b, l, N_HEADS_A, HEAD_DIM_A)
    a_k = a_k.reshape(b, l, N_HEADS_A, HEAD_DIM_A)
    a_v = a_v.reshape(b, l, N_HEADS_A, HEAD_DIM_A)
    ix_q = ix_q.reshape(b, l, IDX_HEADS, IDX_DIM)
    ix_w = ix_w * (IDX_HEADS ** -0.5)

    c_q = rmsnorm(cq, q_lora_g)
    qb = jnp.einsum('blc,chd->blhd', c_q, w_uq)
    qb_nope = qb[..., :QK_NOPE_DIM]
    qb_rope = rope(qb[..., QK_NOPE_DIM:], qpos)
    c_kv = rmsnorm(ckv, kv_lora_g)
    k_rope_new = rope(kr, qpos)

    new_state = (a_k, a_v, ix_k, c_kv, k_rope_new)
    if past is None:
        keys = new_state
    else:
        keys = tuple(jnp.concatenate([pr, nw.astype(pr.dtype)], axis=1) for pr, nw in zip(past, new_state))
    k_all, v_all, ixk_all, ckv_all, kr_all = keys
    kb_nope = jnp.einsum('bsc,chd->bshd', ckv_all, w_uk)
    vb = jnp.einsum('bsc,chd->bshd', ckv_all, w_uv)

    if blocked:
        def block_fn(args):
            aq, iq, iw, qp, bqn, bqr = args
            return (dsa_attend(aq, iq, iw, qp, k_all, v_all, ixk_all, kpos, rel_table, topk),
                    mla_attend(bqn, bqr, qp, kb_nope, kr_all, vb, kpos))
        xs = (to_blocks(a_q), to_blocks(ix_q), to_blocks(ix_w), qpos.reshape(-1, Q_BLOCK),
              to_blocks(qb_nope), to_blocks(qb_rope))
        oa_blk, ob_blk = lax.map(block_fn, xs)
        o_a = from_blocks(oa_blk)
        o_b = from_blocks(ob_blk)
    else:
        o_a = dsa_attend(a_q, ix_q, ix_w, qpos, k_all, v_all, ixk_all, kpos, rel_table, topk)
        o_b = mla_attend(qb_nope, qb_rope, qpos, kb_nope, kr_all, vb, kpos)

    mix = jax.nn.sigmoid(gate_a) * o_a + jax.nn.sigmoid(gate_b) * o_b
    x = x + mix @ w_out
    h2 = rmsnorm(x, norm_ffn_g)
    x = x + jnp.square(jax.nn.relu(h2 @ w_ff_up)) @ w_ff_down
    return x, new_state


def setup_inputs(seed: int = 0) -> dict:
    key = jax.random.key(seed)
    ks = jax.random.split(key, 24)
    f32 = jnp.float32
    nrm = lambda k, shp, s=1.0: jax.random.normal(k, shp, f32) * s
    return {
        "x_prompt": nrm(ks[0], (BATCH, SEQ, D_MODEL)),
        "x_sample": nrm(ks[1], (DEC_BATCH, DEC_SEQ, D_MODEL)),
        "cache_a_k": nrm(ks[2], (DEPTH, DEC_BATCH, PAST_LEN, N_HEADS_A, HEAD_DIM_A)),
        "cache_a_v": nrm(ks[3], (DEPTH, DEC_BATCH, PAST_LEN, N_HEADS_A, HEAD_DIM_A)),
        "cache_a_idx_k": nrm(ks[4], (DEPTH, DEC_BATCH, PAST_LEN, IDX_DIM)),
        "cache_b_ckv": nrm(ks[5], (DEPTH, DEC_BATCH, PAST_LEN, KV_LORA)),
        "cache_b_krope": nrm(ks[6], (DEPTH, DEC_BATCH, PAST_LEN, ROPE_DIM)),
        "rel_bias_table": nrm(ks[7], (REL_BUCKETS, N_HEADS_A), 0.5),
        "norm_mix_g": 1.0 + nrm(ks[8], (DEPTH, D_MODEL), 0.05),
        "w_in": nrm(ks[9], (DEPTH, D_MODEL, IN_COLS), D_MODEL ** -0.5),
        "q_lora_g": 1.0 + nrm(ks[10], (DEPTH, Q_LORA), 0.05),
        "w_uq": nrm(ks[11], (DEPTH, Q_LORA, N_HEADS_B, QK_NOPE_DIM + ROPE_DIM), Q_LORA ** -0.5),
        "kv_lora_g": 1.0 + nrm(ks[12], (DEPTH, KV_LORA), 0.05),
        "w_uk": nrm(ks[13], (DEPTH, KV_LORA, N_HEADS_B, QK_NOPE_DIM), KV_LORA ** -0.5),
        "w_uv": nrm(ks[14], (DEPTH, KV_LORA, N_HEADS_B, V_DIM_B), KV_LORA ** -0.5),
        "w_out": nrm(ks[15], (DEPTH, D_MODEL, D_MODEL), D_MODEL ** -0.5),
        "norm_ffn_g": 1.0 + nrm(ks[16], (DEPTH, D_MODEL), 0.05),
        "w_ff_up": nrm(ks[17], (DEPTH, D_MODEL, D_FF), D_MODEL ** -0.5),
        "w_ff_down": nrm(ks[18], (DEPTH, D_FF, D_MODEL), D_FF ** -0.5),
        "final_norm_g": 1.0 + nrm(ks[19], (D_MODEL,), 0.05),
    }


def reference(x_prompt, x_sample, cache_a_k, cache_a_v, cache_a_idx_k, cache_b_ckv, cache_b_krope,
              rel_bias_table, norm_mix_g, w_in, q_lora_g, w_uq, kv_lora_g, w_uk, w_uv, w_out,
              norm_ffn_g, w_ff_up, w_ff_down, final_norm_g):
    topk_p = min(TOPK_MAX, x_prompt.shape[1] // 4)
    topk_s = min(TOPK_MAX, (cache_a_k.shape[2] + x_sample.shape[1]) // 4)
    hp, hs = x_prompt, x_sample
    st_p, st_s = [], []
    for l in range(DEPTH):
        params = (norm_mix_g[l], w_in[l], q_lora_g[l], w_uq[l], kv_lora_g[l], w_uk[l], w_uv[l],
                  w_out[l], norm_ffn_g[l], w_ff_up[l], w_ff_down[l])
        hp, sp = trunk_layer(hp, None, rel_bias_table, *params, topk=topk_p, blocked=True)
        past = (cache_a_k[l], cache_a_v[l], cache_a_idx_k[l], cache_b_ckv[l], cache_b_krope[l])
        hs, ss = trunk_layer(hs, past, rel_bias_table, *params, topk=topk_s, blocked=False)
        st_p.append(sp)
        st_s.append(ss)
    y_prompt = rmsnorm(hp, final_norm_g)
    y_sample = rmsnorm(hs, final_norm_g)
    a_k_p = jnp.stack([s[0] for s in st_p])
    a_v_p = jnp.stack([s[1] for s in st_p])
    a_idx_p = jnp.stack([s[2] for s in st_p])
    b_ckv_p = jnp.stack([s[3] for s in st_p])
    b_kr_p = jnp.stack([s[4] for s in st_p])
    a_k_s = jnp.stack([s[0] for s in st_s])
    a_v_s = jnp.stack([s[1] for s in st_s])
    a_idx_s = jnp.stack([s[2] for s in st_s])
    b_ckv_s = jnp.stack([s[3] for s in st_s])
    b_kr_s = jnp.stack([s[4] for s in st_s])
    return (y_prompt, y_sample, a_k_p, a_v_p, a_idx_p, b_ckv_p, b_kr_p, a_k_s, a_v_s, a_idx_s, b_ckv_s, b_kr_s)
```

```python
import functools
import math

import numpy as np
import jax
import jax.numpy as jnp
from jax import lax
from jax.experimental import pallas as pl
from jax.experimental.pallas import tpu as pltpu

F32 = jnp.float32
BF16 = jnp.bfloat16
I32 = jnp.int32

CHUNK = 64
CHUNK_SHIFT = 6
N_HEADS = 16
HEAD_DIM = 128
IDX_HEADS = 16
IDX_DIM = 64
TOPK_MAX = 256
REL_BUCKETS = 32
REL_MAX_DIST = 128
QK_NOPE_DIM = 128
ROPE_DIM = 64
Q_LORA = 512
KV_LORA = 256
ROPE_THETA = 10000.0
MLA_SCALE = (QK_NOPE_DIM + ROPE_DIM) ** -0.5
A_SCALE = HEAD_DIM ** -0.5
EPS = 1e-6
NEG_INF = -1e30

LANES = 128
V7X_VMEM_LIMIT = 56 * 1024 * 1024

MISC_IXQ = 0
MISC_CQ = 1024
MISC_CKV = 1536
MISC_KR = 1792
MISC_IXK = 1856
MISC_KRSW = 1920
MISC_IXW = 1984
MISC_COLS = 2048


def _sortable_key_const(v):
    bits = int(np.float32(v).view(np.int32))
    return bits ^ ((bits >> 31) & 0x7FFFFFFF)


KEY_MASKED = _sortable_key_const(NEG_INF)
KEY_HALF = _sortable_key_const(np.float32(0.5) * np.float32(NEG_INF))
INT_MIN = -(2 ** 31)


def _params(sem):
    return pltpu.CompilerParams(dimension_semantics=sem, vmem_limit_bytes=V7X_VMEM_LIMIT)


def _rms(x, g):
    ms = jnp.mean(x * x, axis=-1, keepdims=True)
    return (x * lax.rsqrt(ms + EPS)) * g


def _sort_key(x):
    bits = lax.bitcast_convert_type(x, I32)
    return bits ^ ((bits >> 31) & 0x7FFFFFFF)


def _lane_tile(x, n):
    return x if n == 1 else jnp.concatenate([x] * n, axis=1)


def _norm_cast_kernel(x_ref, g_ref, o_ref):
    o_ref[...] = _rms(x_ref[...], g_ref[...]).astype(o_ref.dtype)


def _norm_cast(x, g, tm):
    m, d = x.shape
    return pl.pallas_call(
        _norm_cast_kernel,
        grid=(m // tm,),
        in_specs=[pl.BlockSpec((tm, d), lambda i: (i, 0)),
                  pl.BlockSpec((1, d), lambda i: (0, 0))],
        out_specs=pl.BlockSpec((tm, d), lambda i: (i, 0)),
        out_shape=jax.ShapeDtypeStruct((m, d), BF16),
        compiler_params=_params(("parallel",)),
        name="norm_cast",
    )(x, g.reshape(1, d))


def _mm_kernel(x_ref, w_ref, *o_refs, mode):
    acc = jnp.dot(x_ref[...], w_ref[...], preferred_element_type=F32)
    if mode == "f32":
        o_refs[0][...] = acc
    elif mode == "sigmoid":
        o_refs[0][...] = jax.nn.sigmoid(acc)
    elif mode == "bf16":
        o_refs[0][...] = acc.astype(BF16)
    else:
        if mode == "f32+heads":
            o_refs[0][...] = acc
        hm = o_refs[-1]
        for h in range(hm.shape[0]):
            hm[h] = acc[:, h * LANES:(h + 1) * LANES].astype(BF16)


def _mm(x, w, mode, tm):
    m, k = x.shape
    n = w.shape[1]
    flat = pl.BlockSpec((tm, n), lambda i: (i, 0))
    heads = pl.BlockSpec((n // LANES, tm, LANES), lambda i: (0, i, 0))
    hm_shape = jax.ShapeDtypeStruct((n // LANES, m, LANES), BF16)
    if mode in ("f32", "sigmoid"):
        out_specs, out_shape = flat, jax.ShapeDtypeStruct((m, n), F32)
    elif mode == "bf16":
        out_specs, out_shape = flat, jax.ShapeDtypeStruct((m, n), BF16)
    elif mode == "heads":
        out_specs, out_shape = heads, hm_shape
    else:
        out_specs = [flat, heads]
        out_shape = [jax.ShapeDtypeStruct((m, n), F32), hm_shape]
    return pl.pallas_call(
        functools.partial(_mm_kernel, mode=mode),
        grid=(m // tm,),
        in_specs=[pl.BlockSpec((tm, k), lambda i: (i, 0)),
                  pl.BlockSpec((k, n), lambda i: (0, 0))],
        out_specs=out_specs,
        out_shape=out_shape,
        compiler_params=_params(("parallel",)),
        name="mm_" + mode.replace("+", "_"),
    )(x, w)


def _mla_prep_kernel(cq_ref, ckv_ref, a_ref, b_ref, cos_ref, sin_ref, gq_ref, gkv_ref,
                     wq_ref, wsw_ref, qcat_ref, ckv_o, ckvb_o, kr_o, krp_o):
    cq = _rms(cq_ref[...], gq_ref[...]).astype(BF16)
    ckv = _rms(ckv_ref[...], gkv_ref[...])
    ckv_o[...] = ckv
    ckvb_o[...] = ckv.astype(BF16)
    cos = cos_ref[...]
    sin = sin_ref[...]
    r = a_ref[...] * cos + b_ref[...] * sin
    kr_o[...] = r[:, :ROPE_DIM]
    krp_o[...] = r.astype(BF16)
    for h in range(N_HEADS):
        qh = jnp.dot(cq, wq_ref[:, h * 256:(h + 1) * 256], preferred_element_type=F32)
        qs = jnp.dot(cq, wsw_ref[:, h * LANES:(h + 1) * LANES], preferred_element_type=F32)
        rp = qh[:, LANES:] * cos + qs * sin
        qcat_ref[h] = jnp.concatenate([qh[:, :LANES], rp], axis=1).astype(BF16)


def _mla_prep(zmisc, cos_t, sin_t, gq, gkv, wq_cat, wq_sw, tm):
    m = zmisc.shape[0]
    row = lambda w, j: pl.BlockSpec((tm, w), lambda i, j=j: (i, j))
    const = lambda a: pl.BlockSpec(a.shape, lambda i: (0,) * a.ndim)
    gq2, gkv2 = gq.reshape(1, Q_LORA), gkv.reshape(1, KV_LORA)
    return pl.pallas_call(
        _mla_prep_kernel,
        grid=(m // tm,),
        in_specs=[row(Q_LORA, MISC_CQ // Q_LORA), row(KV_LORA, MISC_CKV // KV_LORA),
                  row(LANES, MISC_KR // LANES), row(LANES, MISC_KRSW // LANES),
                  row(LANES, 0), row(LANES, 0), const(gq2), const(gkv2),
                  const(wq_cat), const(wq_sw)],
        out_specs=[pl.BlockSpec((N_HEADS, tm, 256), lambda i: (0, i, 0)),
                   row(KV_LORA, 0), row(KV_LORA, 0), row(ROPE_DIM, 0), row(LANES, 0)],
        out_shape=[jax.ShapeDtypeStruct((N_HEADS, m, 256), BF16),
                   jax.ShapeDtypeStruct((m, KV_LORA), F32),
                   jax.ShapeDtypeStruct((m, KV_LORA), BF16),
                   jax.ShapeDtypeStruct((m, ROPE_DIM), F32),
                   jax.ShapeDtypeStruct((m, LANES), BF16)],
        compiler_params=_params(("parallel",)),
        name="mla_prep",
    )(zmisc, zmisc, zmisc, zmisc, cos_t, sin_t, gq2, gkv2, wq_cat, wq_sw)


def _indexer_kernel(q_ref, wb_ref, kT_ref, mask_ref, key_sc, w_sc, *, tq, ts, cw, topk):
    i = pl.program_id(0)
    s_total = kT_ref.shape[1]
    t0 = i * tq
    front = t0 + tq
    nblk = (front + ts - 1) // ts
    nch = nblk * (ts // cw)

    w = wb_ref[...]
    off = MISC_IXW - MISC_KRSW
    for h in range(IDX_HEADS):
        w_sc[h] = jnp.broadcast_to(w[:, off + h:off + h + 1] * (IDX_HEADS ** -0.5), (tq, LANES))
    qb = q_ref[...].astype(BF16)
    qs = [qb[:, h * IDX_DIM:(h + 1) * IDX_DIM] for h in range(IDX_HEADS)]

    def score_block(kb, carry):
        c0 = pl.multiple_of(kb * ts, ts)
        kT = kT_ref[:, pl.ds(c0, ts)]
        acc = jnp.zeros((tq, ts), F32)
        for h in range(IDX_HEADS):
            d = jnp.dot(qs[h], kT, preferred_element_type=F32)
            acc = acc + jnp.maximum(d, 0.0) * _lane_tile(w_sc[h], ts // LANES)
        rows = t0 + lax.broadcasted_iota(I32, (tq, ts), 0)
        cols = c0 + lax.broadcasted_iota(I32, (tq, ts), 1)
        adm = (cols >> CHUNK_SHIFT) <= (rows >> CHUNK_SHIFT)
        key_sc[:, pl.ds(c0, ts)] = jnp.where(adm, _sort_key(acc), KEY_MASKED)
        return carry

    lax.fori_loop(0, nblk, score_block, 0)

    def count_ge(cand):
        def body(c, cnt):
            c0 = pl.multiple_of(c * cw, cw)
            k = key_sc[:, pl.ds(c0, cw)]
            for j in range(cw // LANES):
                cnt = cnt + jnp.where(k[:, j * LANES:(j + 1) * LANES] >= cand, 1.0, 0.0)
            return cnt
        cnt = lax.fori_loop(0, nch, body, jnp.zeros((tq, LANES), F32))
        return jnp.sum(cnt, axis=1, keepdims=True)

    kf = float(topk)
    ans = jnp.where(count_ge(jnp.zeros((tq, LANES), I32)) >= kf, 0, INT_MIN)
    ans = jnp.broadcast_to(ans, (tq, LANES)).astype(I32)

    def bit_step(b, ans):
        cand = ans + jnp.left_shift(jnp.int32(1), 30 - b)
        return jnp.where(count_ge(cand) >= kf, cand, ans)

    ans = lax.fori_loop(0, 31, bit_step, ans)
    thr = _lane_tile(jnp.maximum(ans, KEY_HALF + 1), cw // LANES)

    def write_mask(c, carry):
        c0 = pl.multiple_of(c * cw, cw)
        sel = key_sc[:, pl.ds(c0, cw)] >= thr
        mask_ref[:, pl.ds(c0, cw)] = jnp.where(sel, 1, 0).astype(jnp.int8)
        return carry

    lax.fori_loop(0, nch, write_mask, 0)

    def zero_mask(c, carry):
        c0 = pl.multiple_of(c * cw, cw)
        mask_ref[:, pl.ds(c0, cw)] = jnp.zeros((tq, cw), jnp.int8)
        return carry

    lax.fori_loop(nch, s_total // cw, zero_mask, 0)


def _indexer(zmisc, ixk_t, topk, tq=128, ts=256, cw=256):
    s = ixk_t.shape[1]
    assert ts % cw == 0 and s % ts == 0 and ts % tq == 0
    return pl.pallas_call(
        functools.partial(_indexer_kernel, tq=tq, ts=ts, cw=cw, topk=topk),
        grid=(s // tq,),
        in_specs=[pl.BlockSpec((tq, IDX_HEADS * IDX_DIM), lambda i: (i, 0)),
                  pl.BlockSpec((tq, LANES), lambda i: (i, MISC_KRSW // LANES)),
                  pl.BlockSpec((IDX_DIM, s), lambda i: (0, 0))],
        out_specs=pl.BlockSpec((tq, s), lambda i: (i, 0)),
        out_shape=jax.ShapeDtypeStruct((s, s), jnp.int8),
        scratch_shapes=[pltpu.VMEM((tq, s), I32),
                        pltpu.VMEM((IDX_HEADS, tq, LANES), F32)],
        compiler_params=_params(("parallel",)),
        name="indexer",
    )(zmisc, zmisc, ixk_t)


_NT = (((1,), (1,)), ((), ()))


def _flash_kernel(qi_ref, ki_ref, *refs, kind, tq, tk, scale):
    if kind == "dsa":
        q_ref, k_ref, v_ref, mask_ref, b_ref, o_ref, m_sc, l_sc, acc_sc, mb_sc, s_sc = refs
    else:
        q_ref, k_ref, kr_ref, v_ref, o_ref, m_sc, l_sc, acc_sc, mb_sc = refs
    p = pl.program_id(0)
    qi = qi_ref[p]
    ki = ki_ref[p]
    nrep = tk // LANES

    @pl.when(ki == 0)
    def _init():
        m_sc[...] = jnp.full(m_sc.shape, NEG_INF, F32)
        l_sc[...] = jnp.zeros(l_sc.shape, F32)
        acc_sc[...] = jnp.zeros(acc_sc.shape, F32)

    def update(h, s):
        m_prev = m_sc[h]
        m_new = jnp.maximum(m_prev, jnp.max(s, axis=1, keepdims=True))
        alpha = jnp.exp(m_prev - m_new)
        pr = jnp.exp(s - _lane_tile(m_new, nrep))
        l_sc[h] = alpha * l_sc[h] + jnp.sum(pr, axis=1, keepdims=True)
        acc_sc[h] = alpha * acc_sc[h] + jnp.dot(pr.astype(BF16), v_ref[h],
                                                 preferred_element_type=F32)
        m_sc[h] = m_new

    if kind == "dsa":
        mb_sc[...] = (mask_ref[...].astype(F32) - 1.0) * (-NEG_INF)

        def logits(h):
            s = lax.dot_general(q_ref[h], k_ref[h], _NT, preferred_element_type=F32)
            return s * scale + mb_sc[...]

        def head_far(h, c):
            update(h, logits(h))
            return c

        def head_near(h, c):
            s_sc[...] = logits(h)
            nsub = tq // LANES

            @pl.when(ki == qi)
            def _diag():
                for a in range(nsub):
                    r = slice(a * LANES, (a + 1) * LANES)
                    s_sc[r, r] += b_ref[0, h]
                    if a > 0:
                        s_sc[r, (a - 1) * LANES:a * LANES] += b_ref[1, h]

            @pl.when(ki == qi - 1)
            def _sub():
                s_sc[0:LANES, tk - LANES:tk] += b_ref[1, h]

            update(h, s_sc[...])
            return c

        @pl.when(ki < qi - 1)
        def _far():
            lax.fori_loop(0, N_HEADS, head_far, 0)

        @pl.when(ki >= qi - 1)
        def _near():
            lax.fori_loop(0, N_HEADS, head_near, 0)
    else:
        def logits(h):
            k = jnp.concatenate([k_ref[h], kr_ref[...]], axis=1)
            return lax.dot_general(q_ref[h], k, _NT, preferred_element_type=F32) * scale

        def head_off(h, c):
            update(h, logits(h))
            return c

        def head_diag(h, c):
            update(h, logits(h) + mb_sc[...])
            return c

        @pl.when(ki < qi)
        def _off():
            lax.fori_loop(0, N_HEADS, head_off, 0)

        @pl.when(ki == qi)
        def _diag():
            rows = lax.broadcasted_iota(I32, (tq, tk), 0)
            cols = lax.broadcasted_iota(I32, (tq, tk), 1)
            vis = (cols >> CHUNK_SHIFT) <= (rows >> CHUNK_SHIFT)
            mb_sc[...] = jnp.where(vis, 0.0, NEG_INF)
            lax.fori_loop(0, N_HEADS, head_diag, 0)

    @pl.when(ki == qi)
    def _finish():
        for h in range(N_HEADS):
            o_ref[:, h * LANES:(h + 1) * LANES] = acc_sc[h] / l_sc[h]


def _flash(kind, q, k, v, extra, seq, tb=512):
    nq = seq // tb
    pairs = [(a, b) for a in range(nq) for b in range(a + 1)]
    qi = jnp.asarray([a for a, _ in pairs], I32)
    ki = jnp.asarray([b for _, b in pairs], I32)
    dq = q.shape[2]
    qspec = pl.BlockSpec((N_HEADS, tb, dq), lambda p, qi, ki: (0, qi[p], 0))
    scratch = [pltpu.VMEM((N_HEADS, tb, LANES), F32), pltpu.VMEM((N_HEADS, tb, LANES), F32),
               pltpu.VMEM((N_HEADS, tb, LANES), F32), pltpu.VMEM((tb, tb), F32)]
    if kind == "dsa":
        mask, btiles = extra
        in_specs = [qspec,
                    pl.BlockSpec((N_HEADS, tb, LANES), lambda p, qi, ki: (0, ki[p], 0)),
                    pl.BlockSpec((N_HEADS, tb, LANES), lambda p, qi, ki: (0, ki[p], 0)),
                    pl.BlockSpec((tb, tb), lambda p, qi, ki: (qi[p], ki[p])),
                    pl.BlockSpec(btiles.shape, lambda p, qi, ki: (0, 0, 0, 0))]
        args = (q, k, v, mask, btiles)
        scratch = scratch + [pltpu.VMEM((tb, tb), F32)]
        scale = A_SCALE
    else:
        (krp,) = extra
        in_specs = [qspec,
                    pl.BlockSpec((N_HEADS, tb, LANES), lambda p, qi, ki: (0, ki[p], 0)),
                    pl.BlockSpec((tb, LANES), lambda p, qi, ki: (ki[p], 0)),
                    pl.BlockSpec((N_HEADS, tb, LANES), lambda p, qi, ki: (1, ki[p], 0))]
        args = (q, k, krp, v)
        scale = MLA_SCALE
    return pl.pallas_call(
        functools.partial(_flash_kernel, kind=kind, tq=tb, tk=tb, scale=scale),
        grid_spec=pltpu.PrefetchScalarGridSpec(
            num_scalar_prefetch=2,
            grid=(len(pairs),),
            in_specs=in_specs,
            out_specs=pl.BlockSpec((tb, N_HEADS * LANES), lambda p, qi, ki: (qi[p], 0)),
            scratch_shapes=scratch),
        out_shape=jax.ShapeDtypeStruct((seq, N_HEADS * LANES), F32),
        compiler_params=_params(("arbitrary",)),
        name="flash_" + kind,
    )(qi, ki, *args)


def _sample_kernel(ixq_ref, wb_ref, ikT_ref, aq_ref, ks_ref, vs_ref, bias_ref, qc_ref, kvb_ref,
                   krp_ref, oa_ref, ob_ref, *, n_new, s_valid, past, topk):
    sp = ikT_ref.shape[2]
    qpos = past + lax.broadcasted_iota(I32, (n_new, sp), 0)
    kpos = lax.broadcasted_iota(I32, (n_new, sp), 1)
    adm = (kpos < s_valid) & ((kpos >> CHUNK_SHIFT) <= (qpos >> CHUNK_SHIFT))

    qb = ixq_ref[...].astype(BF16)
    w = wb_ref[...]
    kT = ikT_ref[0]
    off = MISC_IXW - MISC_KRSW
    score = jnp.zeros((n_new, sp), F32)
    for h in range(IDX_HEADS):
        d = jnp.dot(qb[:, h * IDX_DIM:(h + 1) * IDX_DIM], kT, preferred_element_type=F32)
        score = score + jnp.maximum(d, 0.0) * (w[:, off + h:off + h + 1] * (IDX_HEADS ** -0.5))
    key = jnp.where(adm, _sort_key(score), KEY_MASKED)

    kf = float(topk)

    def count_ge(cand):
        return jnp.sum(jnp.where(key >= cand, 1.0, 0.0), axis=1, keepdims=True)

    ans = jnp.where(count_ge(jnp.zeros((n_new, 1), I32)) >= kf, 0, INT_MIN).astype(I32)

    def bit_step(b, ans):
        cand = ans + jnp.left_shift(jnp.int32(1), 30 - b)
        return jnp.where(count_ge(cand) >= kf, cand, ans)

    ans = lax.fori_loop(0, 31, bit_step, ans)
    sel = key >= jnp.maximum(ans, KEY_HALF + 1)

    def attend(s, vmat):
        m = jnp.max(s, axis=1, keepdims=True)
        pr = jnp.exp(s - m)
        l = jnp.sum(pr, axis=1, keepdims=True)
        return jnp.dot(pr.astype(BF16), vmat, preferred_element_type=F32) / l

    for h in range(N_HEADS):
        hs = slice(h * LANES, (h + 1) * LANES)
        s = lax.dot_general(aq_ref[h], ks_ref[0, :, hs], _NT, preferred_element_type=F32)
        s = jnp.where(sel, s * A_SCALE + bias_ref[h], NEG_INF)
        oa_ref[:, hs] = attend(s, vs_ref[0, :, hs])

    krp = krp_ref[0]
    hd = N_HEADS * LANES
    for h in range(N_HEADS):
        hs = slice(h * LANES, (h + 1) * LANES)
        k = jnp.concatenate([kvb_ref[0, :, hs], krp], axis=1)
        s = lax.dot_general(qc_ref[h], k, _NT, preferred_element_type=F32) * MLA_SCALE
        s = jnp.where(adm, s, NEG_INF)
        ob_ref[:, hs] = attend(s, kvb_ref[0, :, hd + h * LANES:hd + (h + 1) * LANES])


def _sample_attn(zmisc, ixk_t, aq_hm, ks, vs, bias, qcat, kvb, krp, n_new, s_valid, past, topk):
    nb, sp = ks.shape[0], ks.shape[1]
    hd = N_HEADS * LANES
    per_stream = lambda a: pl.BlockSpec((1,) + a.shape[1:], lambda b: (b,) + (0,) * (a.ndim - 1))
    return pl.pallas_call(
        functools.partial(_sample_kernel, n_new=n_new, s_valid=s_valid, past=past, topk=topk),
        grid=(nb,),
        in_specs=[pl.BlockSpec((n_new, IDX_HEADS * IDX_DIM), lambda b: (b, 0)),
                  pl.BlockSpec((n_new, LANES), lambda b: (b, MISC_KRSW // LANES)),
                  per_stream(ixk_t),
                  pl.BlockSpec((N_HEADS, n_new, LANES), lambda b: (0, b, 0)),
                  per_stream(ks), per_stream(vs),
                  pl.BlockSpec(bias.shape, lambda b: (0, 0, 0)),
                  pl.BlockSpec((N_HEADS, n_new, 256), lambda b: (0, b, 0)),
                  per_stream(kvb), per_stream(krp)],
        out_specs=[pl.BlockSpec((n_new, hd), lambda b: (b, 0)),
                   pl.BlockSpec((n_new, hd), lambda b: (b, 0))],
        out_shape=[jax.ShapeDtypeStruct((nb * n_new, hd), F32),
                   jax.ShapeDtypeStruct((nb * n_new, hd), F32)],
        compiler_params=_params(("parallel",)),
        name="sample_attn",
    )(zmisc, zmisc, ixk_t, aq_hm, ks, vs, bias, qcat, kvb, krp)


def _outproj_kernel(x_ref, ga_ref, gb_ref, oa_ref, ob_ref, w_ref, o_ref):
    mix = ga_ref[...] * oa_ref[...] + gb_ref[...] * ob_ref[...]
    o_ref[...] = x_ref[...] + jnp.dot(mix.astype(BF16), w_ref[...], preferred_element_type=F32)


def _outproj(x, ga, gb, oa, ob, w, tm):
    m, d = x.shape
    row = pl.BlockSpec((tm, d), lambda i: (i, 0))
    return pl.pallas_call(
        _outproj_kernel,
        grid=(m // tm,),
        in_specs=[row, row, row, row, row, pl.BlockSpec(w.shape, lambda i: (0, 0))],
        out_specs=row,
        out_shape=jax.ShapeDtypeStruct((m, d), F32),
        compiler_params=_params(("parallel",)),
        name="outproj",
    )(x, ga, gb, oa, ob, w)


def _ffn_kernel(x_ref, g_ref, wu_ref, wd_ref, gf_ref, o_ref, h_sc, acc_sc):
    f = pl.program_id(1)

    @pl.when(f == 0)
    def _start():
        h_sc[...] = _rms(x_ref[...], g_ref[...]).astype(BF16)
        acc_sc[...] = jnp.zeros(acc_sc.shape, F32)

    u = jnp.dot(h_sc[...], wu_ref[...], preferred_element_type=F32)
    u = jnp.square(jnp.maximum(u, 0.0)).astype(BF16)
    acc_sc[...] += jnp.dot(u, wd_ref[...], preferred_element_type=F32)

    @pl.when(f == pl.num_programs(1) - 1)
    def _end():
        o_ref[...] = _rms(x_ref[...] + acc_sc[...], gf_ref[...])


def _ffn(x, g, wu, wd, gf, tm, tf=512):
    m, d = x.shape
    dff = wu.shape[1]
    return pl.pallas_call(
        _ffn_kernel,
        grid=(m // tm, dff // tf),
        in_specs=[pl.BlockSpec((tm, d), lambda i, f: (i, 0)),
                  pl.BlockSpec((1, d), lambda i, f: (0, 0)),
                  pl.BlockSpec((d, tf), lambda i, f: (0, f)),
                  pl.BlockSpec((tf, d), lambda i, f: (f, 0)),
                  pl.BlockSpec((1, d), lambda i, f: (0, 0))],
        out_specs=pl.BlockSpec((tm, d), lambda i, f: (i, 0)),
        out_shape=jax.ShapeDtypeStruct((m, d), F32),
        scratch_shapes=[pltpu.VMEM((tm, d), BF16), pltpu.VMEM((tm, d), F32)],
        compiler_params=_params(("parallel", "arbitrary")),
        name="ffn",
    )(x, g.reshape(1, d), wu, wd, gf.reshape(1, d))


def _t5_bucket(rel):
    nb = REL_BUCKETS // 2
    ret = (rel > 0).astype(jnp.int32) * nb
    n = jnp.abs(rel)
    max_exact = nb // 2
    large = max_exact + (jnp.log(jnp.maximum(n, 1).astype(jnp.float32) / max_exact)
                         / math.log(REL_MAX_DIST / max_exact) * (nb - max_exact)).astype(jnp.int32)
    large = jnp.minimum(large, nb - 1)
    return ret + jnp.where(n < max_exact, n, large)


def _rope_tables(pos):
    half = ROPE_DIM // 2
    inv_freq = jnp.power(ROPE_THETA, -jnp.arange(half, dtype=jnp.float32) / half)
    ang = pos.astype(jnp.float32)[:, None] * inv_freq[None, :]
    cos, sin = jnp.cos(ang), jnp.sin(ang)
    pad = jnp.zeros((pos.shape[0], LANES - ROPE_DIM), F32)
    return (jnp.concatenate([cos, cos, pad], axis=1), jnp.concatenate([-sin, sin, pad], axis=1))


def _swap_halves(w):
    half = ROPE_DIM // 2
    return jnp.concatenate([w[..., half:], w[..., :half]], axis=-1)


def _prep_weights(w_in, w_uq, w_uk, w_uv, w_out, w_ff_up, w_ff_down):
    d = w_in.shape[0]
    hd = N_HEADS * HEAD_DIM
    sizes = (hd, hd, hd, IDX_HEADS * IDX_DIM, IDX_DIM, IDX_HEADS, Q_LORA, KV_LORA, ROPE_DIM, d, d)
    cols, off = [], 0
    for n in sizes:
        cols.append(w_in[:, off:off + n])
        off += n
    wq, wk, wv, wixq, wixk, wixw, wcq, wckv, wkr, wga, wgb = cols
    pad = jnp.zeros((d, MISC_COLS - MISC_IXW - IDX_HEADS), w_in.dtype)
    wmisc = jnp.concatenate([wixq, wcq, wckv, wkr, wixk, _swap_halves(wkr), wixw, pad], axis=1)
    nope, rope = w_uq[..., :QK_NOPE_DIM], w_uq[..., QK_NOPE_DIM:]
    zpad = jnp.zeros(rope.shape, w_uq.dtype)
    wq_cat = jnp.concatenate([nope, rope, zpad], axis=-1).reshape(Q_LORA, N_HEADS * 256)
    wq_sw = jnp.concatenate([_swap_halves(rope), zpad], axis=-1).reshape(Q_LORA, N_HEADS * LANES)
    wkv = jnp.concatenate([w_uk.reshape(KV_LORA, hd), w_uv.reshape(KV_LORA, hd)], axis=1)
    c = lambda a: a.astype(BF16)
    return dict(q=c(wq), k=c(wk), v=c(wv), misc=c(wmisc), ga=c(wga), gb=c(wgb), wq_cat=c(wq_cat),
                wq_sw=c(wq_sw), wkv=c(wkv), out=c(w_out), up=c(w_ff_up), down=c(w_ff_down))


def _front(x, pos, w, g_mix, g_q, g_kv, tm):
    h = _norm_cast(x, g_mix, tm)
    aq_hm = _mm(h, w["q"], "heads", tm)
    ak, ak_hm = _mm(h, w["k"], "f32+heads", tm)
    av, av_hm = _mm(h, w["v"], "f32+heads", tm)
    zmisc = _mm(h, w["misc"], "f32", tm)
    sga = _mm(h, w["ga"], "sigmoid", tm)
    sgb = _mm(h, w["gb"], "sigmoid", tm)
    cos_t, sin_t = _rope_tables(pos)
    qcat, ckv, ckv_b, kr, krp = _mla_prep(zmisc, cos_t, sin_t, g_q, g_kv, w["wq_cat"], w["wq_sw"], tm)
    return dict(aq_hm=aq_hm, ak=ak, ak_hm=ak_hm, av=av, av_hm=av_hm, zmisc=zmisc, sga=sga, sgb=sgb,
                qcat=qcat, ckv=ckv, ckv_b=ckv_b, kr=kr, krp=krp)


def _back(x, f, oa, ob, w, g_ffn, g_final, tm_out, tm_ffn):
    x1 = _outproj(x, f["sga"], f["sgb"], oa, ob, w["out"], tm_out)
    return _ffn(x1, g_ffn, w["up"], w["down"], g_final, tm_ffn)


def kernel(x_prompt, x_sample, cache_a_k, cache_a_v, cache_a_idx_k, cache_b_ckv, cache_b_krope,
           rel_bias_table, norm_mix_g, w_in, q_lora_g, w_uq, kv_lora_g, w_uk, w_uv, w_out,
           norm_ffn_g, w_ff_up, w_ff_down, final_norm_g):
    assert w_in.shape[0] == 1, "single-layer trunk"
    _, seq, d = x_prompt.shape
    nb, n_new, _ = x_sample.shape
    past = cache_a_k.shape[2]
    s_valid = past + n_new
    sp = -(-s_valid // LANES) * LANES
    topk_p = min(TOPK_MAX, seq // 4)
    topk_s = min(TOPK_MAX, s_valid // 4)
    hd = N_HEADS * HEAD_DIM

    w = _prep_weights(w_in[0], w_uq[0], w_uk[0], w_uv[0], w_out[0], w_ff_up[0], w_ff_down[0])

    xp = x_prompt.reshape(seq, d)
    fp = _front(xp, jnp.arange(seq, dtype=jnp.int32), w, norm_mix_g[0], q_lora_g[0], kv_lora_g[0], 512)
    ixk_p = fp["zmisc"][:, MISC_IXK:MISC_IXK + IDX_DIM]
    mask = _indexer(fp["zmisc"], ixk_p.astype(BF16).T, topk_p)
    ii = jnp.arange(LANES, dtype=jnp.int32)
    rel0 = ii[None, :] - ii[:, None]
    far = rel_bias_table[_t5_bucket(jnp.full((1, 1), -REL_MAX_DIST, jnp.int32))]
    btiles = jnp.stack([rel_bias_table[_t5_bucket(rel0)] - far,
                        rel_bias_table[_t5_bucket(rel0 - LANES)] - far])
    btiles = jnp.transpose(btiles, (0, 3, 1, 2)).astype(F32)
    oa_p = _flash("dsa", fp["aq_hm"], fp["ak_hm"], fp["av_hm"], (mask, btiles), seq)
    kvb_p = _mm(fp["ckv_b"], w["wkv"], "heads", 512)
    ob_p = _flash("mla", fp["qcat"], kvb_p, kvb_p, (fp["krp"],), seq)
    y_p = _back(xp, fp, oa_p, ob_p, w, norm_ffn_g[0], final_norm_g, 256, 512)

    m_s = nb * n_new
    xs = x_sample.reshape(m_s, d)
    pos_s = jnp.tile(past + jnp.arange(n_new, dtype=jnp.int32), nb)
    fs = _front(xs, pos_s, w, norm_mix_g[0], q_lora_g[0], kv_lora_g[0], m_s)
    ixk_s = fs["zmisc"][:, MISC_IXK:MISC_IXK + IDX_DIM]

    def with_cache(cache, new, width):
        a = jnp.concatenate([cache.reshape(nb, past, width).astype(BF16),
                             new.reshape(nb, n_new, width).astype(BF16)], axis=1)
        return jnp.pad(a, ((0, 0), (0, sp - s_valid), (0, 0)))

    ks = with_cache(cache_a_k[0], fs["ak"], hd)
    vs = with_cache(cache_a_v[0], fs["av"], hd)
    ixk_all = jnp.swapaxes(with_cache(cache_a_idx_k[0], ixk_s, IDX_DIM), 1, 2)
    ckv_all = with_cache(cache_b_ckv[0], fs["ckv"], KV_LORA)
    krope_pad = jnp.pad(cache_b_krope[0], ((0, 0), (0, 0), (0, LANES - ROPE_DIM)))
    krp_all = with_cache(krope_pad, fs["krp"], LANES)
    kvb_s = _mm(ckv_all.reshape(nb * sp, KV_LORA), w["wkv"], "bf16", sp).reshape(nb, sp, 2 * hd)
    qpos = past + jnp.arange(n_new, dtype=jnp.int32)
    kpos = jnp.arange(sp, dtype=jnp.int32)
    bias_s = jnp.transpose(rel_bias_table[_t5_bucket(kpos[None, :] - qpos[:, None])], (2, 0, 1))
    oa_s, ob_s = _sample_attn(fs["zmisc"], ixk_all, fs["aq_hm"], ks, vs, bias_s.astype(F32),
                              fs["qcat"], kvb_s, krp_all, n_new, s_valid, past, topk_s)
    y_s = _back(xs, fs, oa_s, ob_s, w, norm_ffn_g[0], final_norm_g, m_s, m_s)

    st = lambda a, b, *tail: a.reshape((1, b, -1) + tail)
    return (y_p.reshape(1, seq, d), y_s.reshape(nb, n_new, d),
            st(fp["ak"], 1, N_HEADS, HEAD_DIM), st(fp["av"], 1, N_HEADS, HEAD_DIM),
            st(ixk_p, 1, IDX_DIM), st(fp["ckv"], 1, KV_LORA), st(fp["kr"], 1, ROPE_DIM),
            st(fs["ak"], nb, N_HEADS, HEAD_DIM), st(fs["av"], nb, N_HEADS, HEAD_DIM),
            st(ixk_s, nb, IDX_DIM), st(fs["ckv"], nb, KV_LORA), st(fs["kr"], nb, ROPE_DIM))
```

```python
import functools
import math

import numpy as np
import jax
import jax.numpy as jnp
from jax import lax
from jax.experimental import pallas as pl
from jax.experimental.pallas import tpu as pltpu

F32 = jnp.float32
BF16 = jnp.bfloat16
I32 = jnp.int32

CHUNK = 64
CHUNK_SHIFT = 6
N_HEADS = 16
HEAD_DIM = 128
IDX_HEADS = 16
IDX_DIM = 64
TOPK_MAX = 256
REL_BUCKETS = 32
REL_MAX_DIST = 128
QK_NOPE_DIM = 128
ROPE_DIM = 64
Q_LORA = 512
KV_LORA = 256
ROPE_THETA = 10000.0
MLA_SCALE = (QK_NOPE_DIM + ROPE_DIM) ** -0.5
A_SCALE = HEAD_DIM ** -0.5
EPS = 1e-6
NEG_INF = -1e30
LOG2E = math.log2(math.e)

LANES = 128
V7X_VMEM_LIMIT = 56 * 1024 * 1024

MISC_IXQ = 0
MISC_CQ = 1024
MISC_CKV = 1536
MISC_KR = 1792
MISC_IXK = 1856
MISC_KRSW = 1920
MISC_IXW = 1984
MISC_COLS = 2048


def _sortable_key_const(v):
    bits = int(np.float32(v).view(np.int32))
    return bits ^ ((bits >> 31) & 0x7FFFFFFF)


KEY_MASKED = _sortable_key_const(NEG_INF)
KEY_HALF = _sortable_key_const(np.float32(0.5) * np.float32(NEG_INF))
INT_MIN = -(2 ** 31)


def _params(sem):
    return pltpu.CompilerParams(dimension_semantics=sem, vmem_limit_bytes=V7X_VMEM_LIMIT)


def _rms(x, g):
    ms = jnp.mean(x * x, axis=-1, keepdims=True)
    return (x * lax.rsqrt(ms + EPS)) * g


def _sort_key(x):
    bits = lax.bitcast_convert_type(x, I32)
    return bits ^ ((bits >> 31) & 0x7FFFFFFF)


def _lane_tile(x, n):
    return x if n == 1 else jnp.concatenate([x] * n, axis=1)


def _norm_cast_kernel(x_ref, g_ref, o_ref):
    o_ref[...] = _rms(x_ref[...], g_ref[...]).astype(o_ref.dtype)


def _norm_cast(x, g, tm):
    m, d = x.shape
    return pl.pallas_call(
        _norm_cast_kernel,
        grid=(m // tm,),
        in_specs=[pl.BlockSpec((tm, d), lambda i: (i, 0)),
                  pl.BlockSpec((1, d), lambda i: (0, 0))],
        out_specs=pl.BlockSpec((tm, d), lambda i: (i, 0)),
        out_shape=jax.ShapeDtypeStruct((m, d), BF16),
        compiler_params=_params(("parallel",)),
        name="norm_cast",
    )(x, g.reshape(1, d))


def _mm_kernel(x_ref, w_ref, *o_refs, mode, head_scale):
    acc = jnp.dot(x_ref[...], w_ref[...], preferred_element_type=F32)
    if mode == "f32":
        o_refs[0][...] = acc
    elif mode == "sigmoid":
        o_refs[0][...] = jax.nn.sigmoid(acc)
    elif mode == "bf16":
        o_refs[0][...] = acc.astype(BF16)
    else:
        if mode == "f32+heads":
            o_refs[0][...] = acc
        hm = o_refs[-1]
        for h in range(hm.shape[0]):
            blk = acc[:, h * LANES:(h + 1) * LANES]
            hm[h] = (blk if head_scale == 1.0 else blk * head_scale).astype(BF16)


def _mm(x, w, mode, tm, head_scale=1.0):
    m, k = x.shape
    n = w.shape[1]
    flat = pl.BlockSpec((tm, n), lambda i: (i, 0))
    heads = pl.BlockSpec((n // LANES, tm, LANES), lambda i: (0, i, 0))
    hm_shape = jax.ShapeDtypeStruct((n // LANES, m, LANES), BF16)
    if mode in ("f32", "sigmoid"):
        out_specs, out_shape = flat, jax.ShapeDtypeStruct((m, n), F32)
    elif mode == "bf16":
        out_specs, out_shape = flat, jax.ShapeDtypeStruct((m, n), BF16)
    elif mode == "heads":
        out_specs, out_shape = heads, hm_shape
    else:
        out_specs = [flat, heads]
        out_shape = [jax.ShapeDtypeStruct((m, n), F32), hm_shape]
    return pl.pallas_call(
        functools.partial(_mm_kernel, mode=mode, head_scale=head_scale),
        grid=(m // tm,),
        in_specs=[pl.BlockSpec((tm, k), lambda i: (i, 0)),
                  pl.BlockSpec((k, n), lambda i: (0, 0))],
        out_specs=out_specs,
        out_shape=out_shape,
        compiler_params=_params(("parallel",)),
        name="mm_" + mode.replace("+", "_"),
    )(x, w)


def _mla_prep_kernel(cq_ref, ckv_ref, a_ref, b_ref, cos_ref, sin_ref, gq_ref, gkv_ref,
                     wq_ref, wsw_ref, qcat_ref, ckv_o, ckvb_o, kr_o, krp_o):
    cq = _rms(cq_ref[...], gq_ref[...]).astype(BF16)
    ckv = _rms(ckv_ref[...], gkv_ref[...])
    ckv_o[...] = ckv
    ckvb_o[...] = ckv.astype(BF16)
    cos = cos_ref[...]
    sin = sin_ref[...]
    r = a_ref[...] * cos + b_ref[...] * sin
    kr_o[...] = r[:, :ROPE_DIM]
    krp_o[...] = r.astype(BF16)
    for h in range(N_HEADS):
        qh = jnp.dot(cq, wq_ref[:, h * 256:(h + 1) * 256], preferred_element_type=F32)
        qs = jnp.dot(cq, wsw_ref[:, h * LANES:(h + 1) * LANES], preferred_element_type=F32)
        rp = qh[:, LANES:] * cos + qs * sin
        qcat = jnp.concatenate([qh[:, :LANES], rp], axis=1) * (MLA_SCALE * LOG2E)
        qcat_ref[h] = qcat.astype(BF16)


def _mla_prep(zmisc, cos_t, sin_t, gq, gkv, wq_cat, wq_sw, tm):
    m = zmisc.shape[0]
    row = lambda w, j: pl.BlockSpec((tm, w), lambda i, j=j: (i, j))
    const = lambda a: pl.BlockSpec(a.shape, lambda i: (0,) * a.ndim)
    gq2, gkv2 = gq.reshape(1, Q_LORA), gkv.reshape(1, KV_LORA)
    return pl.pallas_call(
        _mla_prep_kernel,
        grid=(m // tm,),
        in_specs=[row(Q_LORA, MISC_CQ // Q_LORA), row(KV_LORA, MISC_CKV // KV_LORA),
                  row(LANES, MISC_KR // LANES), row(LANES, MISC_KRSW // LANES),
                  row(LANES, 0), row(LANES, 0), const(gq2), const(gkv2),
                  const(wq_cat), const(wq_sw)],
        out_specs=[pl.BlockSpec((N_HEADS, tm, 256), lambda i: (0, i, 0)),
                   row(KV_LORA, 0), row(KV_LORA, 0), row(ROPE_DIM, 0), row(LANES, 0)],
        out_shape=[jax.ShapeDtypeStruct((N_HEADS, m, 256), BF16),
                   jax.ShapeDtypeStruct((m, KV_LORA), F32),
                   jax.ShapeDtypeStruct((m, KV_LORA), BF16),
                   jax.ShapeDtypeStruct((m, ROPE_DIM), F32),
                   jax.ShapeDtypeStruct((m, LANES), BF16)],
        compiler_params=_params(("parallel",)),
        name="mla_prep",
    )(zmisc, zmisc, zmisc, zmisc, cos_t, sin_t, gq2, gkv2, wq_cat, wq_sw)


def _indexer_kernel(q_ref, wb_ref, kT_ref, mask_ref, key_sc, w_sc, *, tq, ts, cw, topk):
    i = pl.program_id(0)
    s_total = kT_ref.shape[1]
    t0 = i * tq
    front = t0 + tq
    nblk = (front + ts - 1) // ts
    nch = nblk * (ts // cw)

    w = wb_ref[...]
    off = MISC_IXW - MISC_KRSW
    for h in range(IDX_HEADS):
        w_sc[h] = jnp.broadcast_to(w[:, off + h:off + h + 1] * (IDX_HEADS ** -0.5), (tq, LANES))
    qb = q_ref[...].astype(BF16)
    qs = [qb[:, h * IDX_DIM:(h + 1) * IDX_DIM] for h in range(IDX_HEADS)]

    def score_block(kb, carry):
        c0 = pl.multiple_of(kb * ts, ts)
        kT = kT_ref[:, pl.ds(c0, ts)]
        acc = jnp.zeros((tq, ts), F32)
        for h in range(IDX_HEADS):
            d = jnp.dot(qs[h], kT, preferred_element_type=F32)
            acc = acc + jnp.maximum(d, 0.0) * _lane_tile(w_sc[h], ts // LANES)
        rows = t0 + lax.broadcasted_iota(I32, (tq, ts), 0)
        cols = c0 + lax.broadcasted_iota(I32, (tq, ts), 1)
        adm = (cols >> CHUNK_SHIFT) <= (rows >> CHUNK_SHIFT)
        key_sc[:, pl.ds(c0, ts)] = jnp.where(adm, _sort_key(acc), KEY_MASKED)
        return carry

    lax.fori_loop(0, nblk, score_block, 0)

    def count_ge(cand):
        def body(c, cnt):
            c0 = pl.multiple_of(c * cw, cw)
            k = key_sc[:, pl.ds(c0, cw)]
            for j in range(cw // LANES):
                cnt = cnt + jnp.where(k[:, j * LANES:(j + 1) * LANES] >= cand, 1.0, 0.0)
            return cnt
        cnt = lax.fori_loop(0, nch, body, jnp.zeros((tq, LANES), F32))
        return jnp.sum(cnt, axis=1, keepdims=True)

    kf = float(topk)
    ans = jnp.where(count_ge(jnp.zeros((tq, LANES), I32)) >= kf, 0, INT_MIN)
    ans = jnp.broadcast_to(ans, (tq, LANES)).astype(I32)

    def bit_step(b, ans):
        cand = ans + jnp.left_shift(jnp.int32(1), 30 - b)
        return jnp.where(count_ge(cand) >= kf, cand, ans)

    ans = lax.fori_loop(0, 31, bit_step, ans)
    thr = _lane_tile(jnp.maximum(ans, KEY_HALF + 1), cw // LANES)

    def write_mask(c, carry):
        c0 = pl.multiple_of(c * cw, cw)
        sel = key_sc[:, pl.ds(c0, cw)] >= thr
        mask_ref[:, pl.ds(c0, cw)] = jnp.where(sel, 1, 0).astype(jnp.int8)
        return carry

    lax.fori_loop(0, nch, write_mask, 0)

    def zero_mask(c, carry):
        c0 = pl.multiple_of(c * cw, cw)
        mask_ref[:, pl.ds(c0, cw)] = jnp.zeros((tq, cw), jnp.int8)
        return carry

    lax.fori_loop(nch, s_total // cw, zero_mask, 0)


def _indexer(zmisc, ixk_t, topk, tq=128, ts=256, cw=256):
    s = ixk_t.shape[1]
    assert ts % cw == 0 and s % ts == 0 and ts % tq == 0
    return pl.pallas_call(
        functools.partial(_indexer_kernel, tq=tq, ts=ts, cw=cw, topk=topk),
        grid=(s // tq,),
        in_specs=[pl.BlockSpec((tq, IDX_HEADS * IDX_DIM), lambda i: (i, 0)),
                  pl.BlockSpec((tq, LANES), lambda i: (i, MISC_KRSW // LANES)),
                  pl.BlockSpec((IDX_DIM, s), lambda i: (0, 0))],
        out_specs=pl.BlockSpec((tq, s), lambda i: (i, 0)),
        out_shape=jax.ShapeDtypeStruct((s, s), jnp.int8),
        scratch_shapes=[pltpu.VMEM((tq, s), I32),
                        pltpu.VMEM((IDX_HEADS, tq, LANES), F32)],
        compiler_params=_params(("parallel",)),
        name="indexer",
    )(zmisc, zmisc, ixk_t)


_NT = (((1,), (1,)), ((), ()))
FLASH_GROUP = 8


def _flash_kernel(qi_ref, ki_ref, *refs, kind, tb):
    if kind == "dsa":
        q_ref, k_ref, v_ref, mask_ref, b_ref, o_ref, m_sc, l_sc, acc_sc, mb_sc = refs
    else:
        q_ref, k_ref, kr_ref, v_ref, o_ref, m_sc, l_sc, acc_sc, mb_sc = refs
    p = pl.program_id(0)
    qi = qi_ref[p]
    ki = ki_ref[p]
    nsub = tb // LANES

    @pl.when(ki == 0)
    def _init():
        m_sc[...] = jnp.full(m_sc.shape, NEG_INF, F32)
        l_sc[...] = jnp.zeros(l_sc.shape, F32)
        acc_sc[...] = jnp.zeros(acc_sc.shape, F32)

    ones = jnp.ones((tb, LANES), BF16)

    def update(h, s):
        m_prev = m_sc[h]
        m_new = jnp.maximum(m_prev, jnp.max(s, axis=1, keepdims=True))
        alpha = jnp.exp2(m_prev - m_new)
        pr = jnp.exp2(s - _lane_tile(m_new, nsub)).astype(BF16)
        pv = jnp.dot(pr, jnp.concatenate([v_ref[h], ones], axis=1), preferred_element_type=F32)
        acc_sc[h] = alpha * acc_sc[h] + pv[:, :LANES]
        l_sc[h] = alpha * l_sc[h] + pv[:, LANES:]
        m_sc[h] = m_new

    def run_heads(logits):
        def group(g, carry):
            base = g * FLASH_GROUP
            s_next = logits(base)
            for j in range(FLASH_GROUP):
                s = s_next
                if j + 1 < FLASH_GROUP:
                    s_next = logits(base + j + 1)
                update(base + j, s)
            return carry

        lax.fori_loop(0, N_HEADS // FLASH_GROUP, group, 0)

    if kind == "dsa":
        mb_sc[...] = (mask_ref[...].astype(F32) - 1.0) * (-NEG_INF)

        def logits_far(h):
            return lax.dot_general(q_ref[h], k_ref[h], _NT, preferred_element_type=F32) + mb_sc[...]

        on_diag = ki == qi
        i_main = jnp.where(on_diag, 0, 2)
        i_sub = jnp.where(on_diag, 1, 2)
        i_corner = jnp.where(on_diag, 2, 1)

        def logits_near(h):
            s = logits_far(h)
            rows = []
            for a in range(nsub):
                blocks = []
                for b in range(nsub):
                    blk = s[a * LANES:(a + 1) * LANES, b * LANES:(b + 1) * LANES]
                    if b == a:
                        blk = blk + b_ref[i_main, h]
                    elif b == a - 1:
                        blk = blk + b_ref[i_sub, h]
                    elif a == 0 and b == nsub - 1:
                        blk = blk + b_ref[i_corner, h]
                    blocks.append(blk)
                rows.append(jnp.concatenate(blocks, axis=1))
            return jnp.concatenate(rows, axis=0)

        @pl.when(ki < qi - 1)
        def _far():
            run_heads(logits_far)

        @pl.when(ki >= qi - 1)
        def _near():
            run_heads(logits_near)
    else:
        @pl.when(p == 0)
        def _masks():
            rows = lax.broadcasted_iota(I32, (tb, tb), 0)
            cols = lax.broadcasted_iota(I32, (tb, tb), 1)
            vis = (cols >> CHUNK_SHIFT) <= (rows >> CHUNK_SHIFT)
            mb_sc[...] = jnp.where(vis, 0.0, NEG_INF)

        def qk(h):
            k = jnp.concatenate([k_ref[h], kr_ref[...]], axis=1)
            return lax.dot_general(q_ref[h], k, _NT, preferred_element_type=F32)

        @pl.when(ki < qi)
        def _off():
            run_heads(qk)

        @pl.when(ki == qi)
        def _diag():
            run_heads(lambda h: qk(h) + mb_sc[...])

    @pl.when(ki == qi)
    def _finish():
        for h in range(N_HEADS):
            o_ref[:, h * LANES:(h + 1) * LANES] = acc_sc[h] / l_sc[h]


def _flash(kind, q, k, v, extra, seq, tb=512):
    nq = seq // tb
    pairs = [(a, b) for a in range(nq) for b in range(a + 1)]
    qi = jnp.asarray([a for a, _ in pairs], I32)
    ki = jnp.asarray([b for _, b in pairs], I32)
    dq = q.shape[2]
    qspec = pl.BlockSpec((N_HEADS, tb, dq), lambda p, qi, ki: (0, qi[p], 0))
    scratch = [pltpu.VMEM((N_HEADS, tb, LANES), F32), pltpu.VMEM((N_HEADS, tb, LANES), F32),
               pltpu.VMEM((N_HEADS, tb, LANES), F32)]
    if kind == "dsa":
        mask, btiles = extra
        in_specs = [qspec,
                    pl.BlockSpec((N_HEADS, tb, LANES), lambda p, qi, ki: (0, ki[p], 0)),
                    pl.BlockSpec((N_HEADS, tb, LANES), lambda p, qi, ki: (0, ki[p], 0)),
                    pl.BlockSpec((tb, tb), lambda p, qi, ki: (qi[p], ki[p])),
                    pl.BlockSpec(btiles.shape, lambda p, qi, ki: (0, 0, 0, 0))]
        args = (q, k, v, mask, btiles)
        scratch = scratch + [pltpu.VMEM((tb, tb), F32)]
    else:
        (krp,) = extra
        in_specs = [qspec,
                    pl.BlockSpec((N_HEADS, tb, LANES), lambda p, qi, ki: (0, ki[p], 0)),
                    pl.BlockSpec((tb, LANES), lambda p, qi, ki: (ki[p], 0)),
                    pl.BlockSpec((N_HEADS, tb, LANES), lambda p, qi, ki: (1, ki[p], 0))]
        args = (q, k, krp, v)
        scratch = scratch + [pltpu.VMEM((tb, tb), F32)]
    return pl.pallas_call(
        functools.partial(_flash_kernel, kind=kind, tb=tb),
        grid_spec=pltpu.PrefetchScalarGridSpec(
            num_scalar_prefetch=2,
            grid=(len(pairs),),
            in_specs=in_specs,
            out_specs=pl.BlockSpec((tb, N_HEADS * LANES), lambda p, qi, ki: (qi[p], 0)),
            scratch_shapes=scratch),
        out_shape=jax.ShapeDtypeStruct((seq, N_HEADS * LANES), F32),
        compiler_params=_params(("arbitrary",)),
        name="flash_" + kind,
    )(qi, ki, *args)


def _sample_kernel(ixq_ref, wb_ref, ikT_ref, aq_ref, ks_ref, vs_ref, bias_ref, qc_ref, kvb_ref,
                   krp_ref, oa_ref, ob_ref, *, n_new, s_valid, past, topk):
    sp = ikT_ref.shape[2]
    qpos = past + lax.broadcasted_iota(I32, (n_new, sp), 0)
    kpos = lax.broadcasted_iota(I32, (n_new, sp), 1)
    adm = (kpos < s_valid) & ((kpos >> CHUNK_SHIFT) <= (qpos >> CHUNK_SHIFT))

    qb = ixq_ref[...].astype(BF16)
    w = wb_ref[...]
    kT = ikT_ref[0]
    off = MISC_IXW - MISC_KRSW
    score = jnp.zeros((n_new, sp), F32)
    for h in range(IDX_HEADS):
        d = jnp.dot(qb[:, h * IDX_DIM:(h + 1) * IDX_DIM], kT, preferred_element_type=F32)
        score = score + jnp.maximum(d, 0.0) * (w[:, off + h:off + h + 1] * (IDX_HEADS ** -0.5))
    key = jnp.where(adm, _sort_key(score), KEY_MASKED)

    kf = float(topk)

    def count_ge(cand):
        return jnp.sum(jnp.where(key >= cand, 1.0, 0.0), axis=1, keepdims=True)

    ans = jnp.where(count_ge(jnp.zeros((n_new, 1), I32)) >= kf, 0, INT_MIN).astype(I32)

    def bit_step(b, ans):
        cand = ans + jnp.left_shift(jnp.int32(1), 30 - b)
        return jnp.where(count_ge(cand) >= kf, cand, ans)

    ans = lax.fori_loop(0, 31, bit_step, ans)
    sel = key >= jnp.maximum(ans, KEY_HALF + 1)

    def attend(s, vmat):
        m = jnp.max(s, axis=1, keepdims=True)
        pr = jnp.exp2(s - m)
        l = jnp.sum(pr, axis=1, keepdims=True)
        return jnp.dot(pr.astype(BF16), vmat, preferred_element_type=F32) / l

    for h in range(N_HEADS):
        hs = slice(h * LANES, (h + 1) * LANES)
        s = lax.dot_general(aq_ref[h], ks_ref[0, :, hs], _NT, preferred_element_type=F32)
        s = jnp.where(sel, s + bias_ref[h], NEG_INF)
        oa_ref[:, hs] = attend(s, vs_ref[0, :, hs])

    krp = krp_ref[0]
    hd = N_HEADS * LANES
    for h in range(N_HEADS):
        hs = slice(h * LANES, (h + 1) * LANES)
        k = jnp.concatenate([kvb_ref[0, :, hs], krp], axis=1)
        s = lax.dot_general(qc_ref[h], k, _NT, preferred_element_type=F32)
        s = jnp.where(adm, s, NEG_INF)
        ob_ref[:, hs] = attend(s, kvb_ref[0, :, hd + h * LANES:hd + (h + 1) * LANES])


def _sample_attn(zmisc, ixk_t, aq_hm, ks, vs, bias, qcat, kvb, krp, n_new, s_valid, past, topk):
    nb, sp = ks.shape[0], ks.shape[1]
    hd = N_HEADS * LANES
    per_stream = lambda a: pl.BlockSpec((1,) + a.shape[1:], lambda b: (b,) + (0,) * (a.ndim - 1))
    return pl.pallas_call(
        functools.partial(_sample_kernel, n_new=n_new, s_valid=s_valid, past=past, topk=topk),
        grid=(nb,),
        in_specs=[pl.BlockSpec((n_new, IDX_HEADS * IDX_DIM), lambda b: (b, 0)),
                  pl.BlockSpec((n_new, LANES), lambda b: (b, MISC_KRSW // LANES)),
                  per_stream(ixk_t),
                  pl.BlockSpec((N_HEADS, n_new, LANES), lambda b: (0, b, 0)),
                  per_stream(ks), per_stream(vs),
                  pl.BlockSpec(bias.shape, lambda b: (0, 0, 0)),
                  pl.BlockSpec((N_HEADS, n_new, 256), lambda b: (0, b, 0)),
                  per_stream(kvb), per_stream(krp)],
        out_specs=[pl.BlockSpec((n_new, hd), lambda b: (b, 0)),
                   pl.BlockSpec((n_new, hd), lambda b: (b, 0))],
        out_shape=[jax.ShapeDtypeStruct((nb * n_new, hd), F32),
                   jax.ShapeDtypeStruct((nb * n_new, hd), F32)],
        compiler_params=_params(("parallel",)),
        name="sample_attn",
    )(zmisc, zmisc, ixk_t, aq_hm, ks, vs, bias, qcat, kvb, krp)


def _outproj_kernel(x_ref, ga_ref, gb_ref, oa_ref, ob_ref, w_ref, o_ref):
    mix = ga_ref[...] * oa_ref[...] + gb_ref[...] * ob_ref[...]
    o_ref[...] = x_ref[...] + jnp.dot(mix.astype(BF16), w_ref[...], preferred_element_type=F32)


def _outproj(x, ga, gb, oa, ob, w, tm):
    m, d = x.shape
    row = pl.BlockSpec((tm, d), lambda i: (i, 0))
    return pl.pallas_call(
        _outproj_kernel,
        grid=(m // tm,),
        in_specs=[row, row, row, row, row, pl.BlockSpec(w.shape, lambda i: (0, 0))],
        out_specs=row,
        out_shape=jax.ShapeDtypeStruct((m, d), F32),
        compiler_params=_params(("parallel",)),
        name="outproj",
    )(x, ga, gb, oa, ob, w)


def _ffn_kernel(x_ref, g_ref, wu_ref, wd_ref, gf_ref, o_ref, h_sc, acc_sc):
    f = pl.program_id(1)

    @pl.when(f == 0)
    def _start():
        h_sc[...] = _rms(x_ref[...], g_ref[...]).astype(BF16)
        acc_sc[...] = jnp.zeros(acc_sc.shape, F32)

    u = jnp.dot(h_sc[...], wu_ref[...], preferred_element_type=F32)
    u = jnp.square(jnp.maximum(u, 0.0)).astype(BF16)
    acc_sc[...] += jnp.dot(u, wd_ref[...], preferred_element_type=F32)

    @pl.when(f == pl.num_programs(1) - 1)
    def _end():
        o_ref[...] = _rms(x_ref[...] + acc_sc[...], gf_ref[...])


def _ffn(x, g, wu, wd, gf, tm, tf=512):
    m, d = x.shape
    dff = wu.shape[1]
    return pl.pallas_call(
        _ffn_kernel,
        grid=(m // tm, dff // tf),
        in_specs=[pl.BlockSpec((tm, d), lambda i, f: (i, 0)),
                  pl.BlockSpec((1, d), lambda i, f: (0, 0)),
                  pl.BlockSpec((d, tf), lambda i, f: (0, f)),
                  pl.BlockSpec((tf, d), lambda i, f: (f, 0)),
                  pl.BlockSpec((1, d), lambda i, f: (0, 0))],
        out_specs=pl.BlockSpec((tm, d), lambda i, f: (i, 0)),
        out_shape=jax.ShapeDtypeStruct((m, d), F32),
        scratch_shapes=[pltpu.VMEM((tm, d), BF16), pltpu.VMEM((tm, d), F32)],
        compiler_params=_params(("parallel", "arbitrary")),
        name="ffn",
    )(x, g.reshape(1, d), wu, wd, gf.reshape(1, d))


def _t5_bucket(rel):
    nb = REL_BUCKETS // 2
    ret = (rel > 0).astype(jnp.int32) * nb
    n = jnp.abs(rel)
    max_exact = nb // 2
    large = max_exact + (jnp.log(jnp.maximum(n, 1).astype(jnp.float32) / max_exact)
                         / math.log(REL_MAX_DIST / max_exact) * (nb - max_exact)).astype(jnp.int32)
    large = jnp.minimum(large, nb - 1)
    return ret + jnp.where(n < max_exact, n, large)


def _rope_tables(pos):
    half = ROPE_DIM // 2
    inv_freq = jnp.power(ROPE_THETA, -jnp.arange(half, dtype=jnp.float32) / half)
    ang = pos.astype(jnp.float32)[:, None] * inv_freq[None, :]
    cos, sin = jnp.cos(ang), jnp.sin(ang)
    pad = jnp.zeros((pos.shape[0], LANES - ROPE_DIM), F32)
    return (jnp.concatenate([cos, cos, pad], axis=1), jnp.concatenate([-sin, sin, pad], axis=1))


def _swap_halves(w):
    half = ROPE_DIM // 2
    return jnp.concatenate([w[..., half:], w[..., :half]], axis=-1)


def _prep_weights(w_in, w_uq, w_uk, w_uv, w_out, w_ff_up, w_ff_down):
    d = w_in.shape[0]
    hd = N_HEADS * HEAD_DIM
    sizes = (hd, hd, hd, IDX_HEADS * IDX_DIM, IDX_DIM, IDX_HEADS, Q_LORA, KV_LORA, ROPE_DIM, d, d)
    cols, off = [], 0
    for n in sizes:
        cols.append(w_in[:, off:off + n])
        off += n
    wq, wk, wv, wixq, wixk, wixw, wcq, wckv, wkr, wga, wgb = cols
    pad = jnp.zeros((d, MISC_COLS - MISC_IXW - IDX_HEADS), w_in.dtype)
    wmisc = jnp.concatenate([wixq, wcq, wckv, wkr, wixk, _swap_halves(wkr), wixw, pad], axis=1)
    nope, rope = w_uq[..., :QK_NOPE_DIM], w_uq[..., QK_NOPE_DIM:]
    zpad = jnp.zeros(rope.shape, w_uq.dtype)
    wq_cat = jnp.concatenate([nope, rope, zpad], axis=-1).reshape(Q_LORA, N_HEADS * 256)
    wq_sw = jnp.concatenate([_swap_halves(rope), zpad], axis=-1).reshape(Q_LORA, N_HEADS * LANES)
    wkv = jnp.concatenate([w_uk.reshape(KV_LORA, hd), w_uv.reshape(KV_LORA, hd)], axis=1)
    c = lambda a: a.astype(BF16)
    return dict(q=c(wq), k=c(wk), v=c(wv), misc=c(wmisc), ga=c(wga), gb=c(wgb), wq_cat=c(wq_cat),
                wq_sw=c(wq_sw), wkv=c(wkv), out=c(w_out), up=c(w_ff_up), down=c(w_ff_down))


def _front(x, pos, w, g_mix, g_q, g_kv, tm):
    h = _norm_cast(x, g_mix, tm)
    aq_hm = _mm(h, w["q"], "heads", tm, head_scale=A_SCALE * LOG2E)
    ak, ak_hm = _mm(h, w["k"], "f32+heads", tm)
    av, av_hm = _mm(h, w["v"], "f32+heads", tm)
    zmisc = _mm(h, w["misc"], "f32", tm)
    sga = _mm(h, w["ga"], "sigmoid", tm)
    sgb = _mm(h, w["gb"], "sigmoid", tm)
    cos_t, sin_t = _rope_tables(pos)
    qcat, ckv, ckv_b, kr, krp = _mla_prep(zmisc, cos_t, sin_t, g_q, g_kv, w["wq_cat"], w["wq_sw"], tm)
    return dict(aq_hm=aq_hm, ak=ak, ak_hm=ak_hm, av=av, av_hm=av_hm, zmisc=zmisc, sga=sga, sgb=sgb,
                qcat=qcat, ckv=ckv, ckv_b=ckv_b, kr=kr, krp=krp)


def _back(x, f, oa, ob, w, g_ffn, g_final, tm_out, tm_ffn):
    x1 = _outproj(x, f["sga"], f["sgb"], oa, ob, w["out"], tm_out)
    return _ffn(x1, g_ffn, w["up"], w["down"], g_final, tm_ffn)


def kernel(x_prompt, x_sample, cache_a_k, cache_a_v, cache_a_idx_k, cache_b_ckv, cache_b_krope,
           rel_bias_table, norm_mix_g, w_in, q_lora_g, w_uq, kv_lora_g, w_uk, w_uv, w_out,
           norm_ffn_g, w_ff_up, w_ff_down, final_norm_g):
    assert w_in.shape[0] == 1, "single-layer trunk"
    _, seq, d = x_prompt.shape
    nb, n_new, _ = x_sample.shape
    past = cache_a_k.shape[2]
    s_valid = past + n_new
    sp = -(-s_valid // LANES) * LANES
    topk_p = min(TOPK_MAX, seq // 4)
    topk_s = min(TOPK_MAX, s_valid // 4)
    hd = N_HEADS * HEAD_DIM

    w = _prep_weights(w_in[0], w_uq[0], w_uk[0], w_uv[0], w_out[0], w_ff_up[0], w_ff_down[0])

    xp = x_prompt.reshape(seq, d)
    fp = _front(xp, jnp.arange(seq, dtype=jnp.int32), w, norm_mix_g[0], q_lora_g[0], kv_lora_g[0], 512)
    ixk_p = fp["zmisc"][:, MISC_IXK:MISC_IXK + IDX_DIM]
    mask = _indexer(fp["zmisc"], ixk_p.astype(BF16).T, topk_p)
    ii = jnp.arange(LANES, dtype=jnp.int32)
    rel0 = ii[None, :] - ii[:, None]
    far = rel_bias_table[_t5_bucket(jnp.full((1, 1), -REL_MAX_DIST, jnp.int32))]
    btiles = jnp.stack([rel_bias_table[_t5_bucket(rel0)] - far,
                        rel_bias_table[_t5_bucket(rel0 - LANES)] - far,
                        jnp.zeros((LANES, LANES, N_HEADS), F32)])
    btiles = jnp.transpose(btiles, (0, 3, 1, 2)).astype(F32) * LOG2E
    oa_p = _flash("dsa", fp["aq_hm"], fp["ak_hm"], fp["av_hm"], (mask, btiles), seq)
    kvb_p = _mm(fp["ckv_b"], w["wkv"], "heads", 512)
    ob_p = _flash("mla", fp["qcat"], kvb_p, kvb_p, (fp["krp"],), seq)
    y_p = _back(xp, fp, oa_p, ob_p, w, norm_ffn_g[0], final_norm_g, 256, 512)

    m_s = nb * n_new
    xs = x_sample.reshape(m_s, d)
    pos_s = jnp.tile(past + jnp.arange(n_new, dtype=jnp.int32), nb)
    fs = _front(xs, pos_s, w, norm_mix_g[0], q_lora_g[0], kv_lora_g[0], m_s)
    ixk_s = fs["zmisc"][:, MISC_IXK:MISC_IXK + IDX_DIM]

    def with_cache(cache, new, width):
        a = jnp.concatenate([cache.reshape(nb, past, width).astype(BF16),
                             new.reshape(nb, n_new, width).astype(BF16)], axis=1)
        return jnp.pad(a, ((0, 0), (0, sp - s_valid), (0, 0)))

    ks = with_cache(cache_a_k[0], fs["ak"], hd)
    vs = with_cache(cache_a_v[0], fs["av"], hd)
    ixk_all = jnp.swapaxes(with_cache(cache_a_idx_k[0], ixk_s, IDX_DIM), 1, 2)
    ckv_all = with_cache(cache_b_ckv[0], fs["ckv"], KV_LORA)
    krope_pad = jnp.pad(cache_b_krope[0], ((0, 0), (0, 0), (0, LANES - ROPE_DIM)))
    krp_all = with_cache(krope_pad, fs["krp"], LANES)
    kvb_s = _mm(ckv_all.reshape(nb * sp, KV_LORA), w["wkv"], "bf16", sp).reshape(nb, sp, 2 * hd)
    qpos = past + jnp.arange(n_new, dtype=jnp.int32)
    kpos = jnp.arange(sp, dtype=jnp.int32)
    bias_s = jnp.transpose(rel_bias_table[_t5_bucket(kpos[None, :] - qpos[:, None])], (2, 0, 1))
    oa_s, ob_s = _sample_attn(fs["zmisc"], ixk_all, fs["aq_hm"], ks, vs, bias_s.astype(F32) * LOG2E,
                              fs["qcat"], kvb_s, krp_all, n_new, s_valid, past, topk_s)
    y_s = _back(xs, fs, oa_s, ob_s, w, norm_ffn_g[0], final_norm_g, m_s, m_s)

    st = lambda a, b, *tail: a.reshape((1, b, -1) + tail)
    return (y_p.reshape(1, seq, d), y_s.reshape(nb, n_new, d),
            st(fp["ak"], 1, N_HEADS, HEAD_DIM), st(fp["av"], 1, N_HEADS, HEAD_DIM),
            st(ixk_p, 1, IDX_DIM), st(fp["ckv"], 1, KV_LORA), st(fp["kr"], 1, ROPE_DIM),
            st(fs["ak"], nb, N_HEADS, HEAD_DIM), st(fs["av"], nb, N_HEADS, HEAD_DIM),
            st(ixk_s, nb, IDX_DIM), st(fs["ckv"], nb, KV_LORA), st(fs["kr"], nb, ROPE_DIM))
```

```python
import functools
import math

import numpy as np
import jax
import jax.numpy as jnp
from jax import lax
from jax.experimental import pallas as pl
from jax.experimental.pallas import tpu as pltpu

F32 = jnp.float32
BF16 = jnp.bfloat16
I32 = jnp.int32

CHUNK = 64
CHUNK_SHIFT = 6
N_HEADS = 16
HEAD_DIM = 128
IDX_HEADS = 16
IDX_DIM = 64
TOPK_MAX = 256
REL_BUCKETS = 32
REL_MAX_DIST = 128
QK_NOPE_DIM = 128
ROPE_DIM = 64
Q_LORA = 512
KV_LORA = 256
ROPE_THETA = 10000.0
MLA_SCALE = (QK_NOPE_DIM + ROPE_DIM) ** -0.5
A_SCALE = HEAD_DIM ** -0.5
EPS = 1e-6
NEG_INF = -1e30
LOG2E = math.log2(math.e)

LANES = 128
V7X_VMEM_LIMIT = 56 * 1024 * 1024

MISC_IXQ = 0
MISC_CQ = 1024
MISC_CKV = 1536
MISC_KR = 1792
MISC_IXK = 1856
MISC_KRSW = 1920
MISC_IXW = 1984
MISC_COLS = 2048


def _sortable_key_const(v):
    bits = int(np.float32(v).view(np.int32))
    return bits ^ ((bits >> 31) & 0x7FFFFFFF)


KEY_MASKED = _sortable_key_const(NEG_INF)
KEY_HALF = _sortable_key_const(np.float32(0.5) * np.float32(NEG_INF))
INT_MIN = -(2 ** 31)
HI16_MASK = -65536


def _params(sem):
    return pltpu.CompilerParams(dimension_semantics=sem, vmem_limit_bytes=V7X_VMEM_LIMIT)


def _rms(x, g):
    ms = jnp.mean(x * x, axis=-1, keepdims=True)
    return (x * lax.rsqrt(ms + EPS)) * g


def _sort_key(x):
    bits = lax.bitcast_convert_type(x, I32)
    return bits ^ ((bits >> 31) & 0x7FFFFFFF)


def _lane_tile(x, n):
    return x if n == 1 else jnp.concatenate([x] * n, axis=1)


def _norm_cast_kernel(x_ref, g_ref, o_ref):
    o_ref[...] = _rms(x_ref[...], g_ref[...]).astype(o_ref.dtype)


def _norm_cast(x, g, tm):
    m, d = x.shape
    return pl.pallas_call(
        _norm_cast_kernel,
        grid=(m // tm,),
        in_specs=[pl.BlockSpec((tm, d), lambda i: (i, 0)),
                  pl.BlockSpec((1, d), lambda i: (0, 0))],
        out_specs=pl.BlockSpec((tm, d), lambda i: (i, 0)),
        out_shape=jax.ShapeDtypeStruct((m, d), BF16),
        compiler_params=_params(("parallel",)),
        name="norm_cast",
    )(x, g.reshape(1, d))


def _mm_kernel(x_ref, w_ref, *o_refs, mode, head_scale):
    acc = jnp.dot(x_ref[...], w_ref[...], preferred_element_type=F32)
    if mode == "f32":
        o_refs[0][...] = acc
    elif mode == "sigmoid":
        o_refs[0][...] = jax.nn.sigmoid(acc)
    elif mode == "bf16":
        o_refs[0][...] = acc.astype(BF16)
    else:
        hm = o_refs[-1]
        for h in range(hm.shape[0]):
            blk = acc[:, h * LANES:(h + 1) * LANES]
            if mode == "f32+heads":
                o_refs[0][:, h, :] = blk
            hm[h] = (blk if head_scale == 1.0 else blk * head_scale).astype(BF16)


def _mm(x, w, mode, tm, head_scale=1.0):
    m, k = x.shape
    n = w.shape[1]
    flat = pl.BlockSpec((tm, n), lambda i: (i, 0))
    heads = pl.BlockSpec((n // LANES, tm, LANES), lambda i: (0, i, 0))
    hm_shape = jax.ShapeDtypeStruct((n // LANES, m, LANES), BF16)
    if mode in ("f32", "sigmoid"):
        out_specs, out_shape = flat, jax.ShapeDtypeStruct((m, n), F32)
    elif mode == "bf16":
        out_specs, out_shape = flat, jax.ShapeDtypeStruct((m, n), BF16)
    elif mode == "heads":
        out_specs, out_shape = heads, hm_shape
    else:
        out_specs = [pl.BlockSpec((tm, n // LANES, LANES), lambda i: (i, 0, 0)), heads]
        out_shape = [jax.ShapeDtypeStruct((m, n // LANES, LANES), F32), hm_shape]
    return pl.pallas_call(
        functools.partial(_mm_kernel, mode=mode, head_scale=head_scale),
        grid=(m // tm,),
        in_specs=[pl.BlockSpec((tm, k), lambda i: (i, 0)),
                  pl.BlockSpec((k, n), lambda i: (0, 0))],
        out_specs=out_specs,
        out_shape=out_shape,
        compiler_params=_params(("parallel",)),
        name="mm_" + mode.replace("+", "_"),
    )(x, w)


def _mla_prep_kernel(cq_ref, ckv_ref, a_ref, b_ref, cos_ref, sin_ref, gq_ref, gkv_ref,
                     wq_ref, wsw_ref, qcat_ref, ckv_o, ckvb_o, kr_o, krp_o):
    cq = _rms(cq_ref[...], gq_ref[...]).astype(BF16)
    ckv = _rms(ckv_ref[...], gkv_ref[...])
    ckv_o[...] = ckv
    ckvb_o[...] = ckv.astype(BF16)
    cos = cos_ref[...]
    sin = sin_ref[...]
    r = a_ref[...] * cos + b_ref[...] * sin
    kr_o[...] = r[:, :ROPE_DIM]
    krp_o[...] = r.astype(BF16)
    for h in range(N_HEADS):
        qh = jnp.dot(cq, wq_ref[:, h * 256:(h + 1) * 256], preferred_element_type=F32)
        qs = jnp.dot(cq, wsw_ref[:, h * LANES:(h + 1) * LANES], preferred_element_type=F32)
        rp = qh[:, LANES:] * cos + qs * sin
        qcat = jnp.concatenate([qh[:, :LANES], rp], axis=1) * (MLA_SCALE * LOG2E)
        qcat_ref[h] = qcat.astype(BF16)


def _mla_prep(zmisc, cos_t, sin_t, gq, gkv, wq_cat, wq_sw, tm):
    m = zmisc.shape[0]
    row = lambda w, j: pl.BlockSpec((tm, w), lambda i, j=j: (i, j))
    const = lambda a: pl.BlockSpec(a.shape, lambda i: (0,) * a.ndim)
    gq2, gkv2 = gq.reshape(1, Q_LORA), gkv.reshape(1, KV_LORA)
    return pl.pallas_call(
        _mla_prep_kernel,
        grid=(m // tm,),
        in_specs=[row(Q_LORA, MISC_CQ // Q_LORA), row(KV_LORA, MISC_CKV // KV_LORA),
                  row(LANES, MISC_KR // LANES), row(LANES, MISC_KRSW // LANES),
                  row(LANES, 0), row(LANES, 0), const(gq2), const(gkv2),
                  const(wq_cat), const(wq_sw)],
        out_specs=[pl.BlockSpec((N_HEADS, tm, 256), lambda i: (0, i, 0)),
                   row(KV_LORA, 0), row(KV_LORA, 0), row(ROPE_DIM, 0), row(LANES, 0)],
        out_shape=[jax.ShapeDtypeStruct((N_HEADS, m, 256), BF16),
                   jax.ShapeDtypeStruct((m, KV_LORA), F32),
                   jax.ShapeDtypeStruct((m, KV_LORA), BF16),
                   jax.ShapeDtypeStruct((m, ROPE_DIM), F32),
                   jax.ShapeDtypeStruct((m, LANES), BF16)],
        compiler_params=_params(("parallel",)),
        name="mla_prep",
    )(zmisc, zmisc, zmisc, zmisc, cos_t, sin_t, gq2, gkv2, wq_cat, wq_sw)


def _indexer_kernel(q_ref, wb_ref, kT_ref, mask_ref, key_sc, co_sc, w_sc, *, tq, ts, cw, topk):
    i = pl.program_id(0)
    s_total = kT_ref.shape[1]
    t0 = i * tq
    front = t0 + tq
    nblk = (front + ts - 1) // ts
    nch = nblk * (ts // cw)

    w = wb_ref[...]
    off = MISC_IXW - MISC_KRSW
    for h in range(IDX_HEADS):
        w_sc[h] = jnp.broadcast_to(w[:, off + h:off + h + 1] * (IDX_HEADS ** -0.5), (tq, LANES))
    qb = q_ref[...].astype(BF16)
    qs = [qb[:, h * IDX_DIM:(h + 1) * IDX_DIM] for h in range(IDX_HEADS)]

    def score_block(kb, carry):
        c0 = pl.multiple_of(kb * ts, ts)
        kT = kT_ref[:, pl.ds(c0, ts)]
        acc = jnp.zeros((tq, ts), F32)
        for h in range(IDX_HEADS):
            d = jnp.dot(qs[h], kT, preferred_element_type=F32)
            acc = acc + jnp.maximum(d, 0.0) * _lane_tile(w_sc[h], ts // LANES)
        rows = t0 + lax.broadcasted_iota(I32, (tq, ts), 0)
        cols = c0 + lax.broadcasted_iota(I32, (tq, ts), 1)
        adm = (cols >> CHUNK_SHIFT) <= (rows >> CHUNK_SHIFT)
        sc = jnp.where(adm, acc, NEG_INF)
        bits = lax.bitcast_convert_type(sc, I32)
        key_sc[:, pl.ds(c0, ts)] = bits ^ ((bits >> 31) & 0x7FFFFFFF)
        hi = lax.bitcast_convert_type(bits & HI16_MASK, F32)
        co_sc[:, pl.ds(c0, ts)] = hi.astype(BF16)
        return carry

    lax.fori_loop(0, nblk, score_block, 0)

    kf = float(topk)

    def count_ge(cand):
        def body(c, cnt):
            c0 = pl.multiple_of(c * cw, cw)
            k = key_sc[:, pl.ds(c0, cw)]
            for j in range(cw // LANES):
                cnt = cnt + jnp.where(k[:, j * LANES:(j + 1) * LANES] >= cand, 1.0, 0.0)
            return cnt
        cnt = lax.fori_loop(0, nch, body, jnp.zeros((tq, LANES), F32))
        return jnp.broadcast_to(jnp.sum(cnt, axis=1, keepdims=True), (tq, LANES))

    def pattern16(c16):
        return jnp.where(c16 >= 0, c16, c16 ^ 0x7FFF) & 0xFFFF

    def count_ge_coarse(c16):
        cand = lax.bitcast_convert_type(pattern16(c16) << 16, F32).astype(BF16)

        def body(c, cnt):
            c0 = pl.multiple_of(c * cw, cw)
            x = co_sc[:, pl.ds(c0, cw)]
            for j in range(cw // LANES):
                hit = x[:, j * LANES:(j + 1) * LANES] >= cand
                cnt = cnt + jnp.where(hit, jnp.ones((), BF16), jnp.zeros((), BF16))
            return cnt
        cnt = lax.fori_loop(0, nch, body, jnp.zeros((tq, LANES), BF16))
        total = jnp.sum(cnt.astype(F32), axis=1, keepdims=True)
        return jnp.broadcast_to(total, (tq, LANES))

    c0cnt = count_ge_coarse(jnp.zeros((tq, LANES), I32))
    pos = c0cnt >= kf
    t16 = jnp.where(pos, 0, -32768).astype(I32)
    cnt_lo = jnp.where(pos, c0cnt, float(s_total))

    def coarse_step(b, st):
        t16, cnt_lo = st
        cand = t16 + jnp.left_shift(jnp.int32(1), 14 - b)
        cnt = count_ge_coarse(cand)
        ok = cnt >= kf
        return jnp.where(ok, cand, t16), jnp.where(ok, cnt, cnt_lo)

    t16, cnt_lo = lax.fori_loop(0, 15, coarse_step, (t16, cnt_lo))

    pat = pattern16(t16)
    b0 = pat << 16
    b1 = b0 | 0xFFFF
    k0 = b0 ^ ((b0 >> 31) & 0x7FFFFFFF)
    k1 = b1 ^ ((b1 >> 31) & 0x7FFFFFFF)
    is_zero = (pat & 0x7FFF) == 0
    lo = jnp.where(is_zero, -65536, jnp.minimum(k0, k1))
    hi = jnp.where(is_zero, 65535, jnp.maximum(k0, k1))
    rows1 = t0 + lax.broadcasted_iota(I32, (tq, LANES), 0)
    few = (((rows1 >> CHUNK_SHIFT) + 1) << CHUNK_SHIFT) < topk

    def active_of(lo, hi, cnt_lo):
        done = few | (cnt_lo == kf) | (lo >= hi)
        return jnp.max(jnp.where(done, 0.0, 1.0))

    def fine_cond(st):
        return st[3] > 0.5

    def fine_step(st):
        lo, hi, cnt_lo, _ = st
        done = few | (cnt_lo == kf) | (lo >= hi)
        mid = lo + ((hi - lo + 1) >> 1)
        cnt = count_ge(mid)
        up = (cnt >= kf) & jnp.logical_not(done)
        down = (cnt < kf) & jnp.logical_not(done)
        lo = jnp.where(up, mid, lo)
        cnt_lo = jnp.where(up, cnt, cnt_lo)
        hi = jnp.where(down, mid - 1, hi)
        return lo, hi, cnt_lo, active_of(lo, hi, cnt_lo)

    lo, hi, cnt_lo, _ = lax.while_loop(fine_cond, fine_step, (lo, hi, cnt_lo, active_of(lo, hi, cnt_lo)))
    ans = jnp.where(few, INT_MIN, lo)
    thr = _lane_tile(jnp.maximum(ans, KEY_HALF + 1), cw // LANES)

    def write_mask(c, carry):
        c0 = pl.multiple_of(c * cw, cw)
        sel = key_sc[:, pl.ds(c0, cw)] >= thr
        mask_ref[:, pl.ds(c0, cw)] = jnp.where(sel, 1, 0).astype(jnp.int8)
        return carry

    lax.fori_loop(0, nch, write_mask, 0)

    def zero_mask(c, carry):
        c0 = pl.multiple_of(c * cw, cw)
        mask_ref[:, pl.ds(c0, cw)] = jnp.zeros((tq, cw), jnp.int8)
        return carry

    lax.fori_loop(nch, s_total // cw, zero_mask, 0)


def _indexer(zmisc, ixk_t, topk, tq=128, ts=512, cw=512):
    s = ixk_t.shape[1]
    assert ts % cw == 0 and s % ts == 0 and ts % tq == 0
    return pl.pallas_call(
        functools.partial(_indexer_kernel, tq=tq, ts=ts, cw=cw, topk=topk),
        grid=(s // tq,),
        in_specs=[pl.BlockSpec((tq, IDX_HEADS * IDX_DIM), lambda i: (i, 0)),
                  pl.BlockSpec((tq, LANES), lambda i: (i, MISC_KRSW // LANES)),
                  pl.BlockSpec((IDX_DIM, s), lambda i: (0, 0))],
        out_specs=pl.BlockSpec((tq, s), lambda i: (i, 0)),
        out_shape=jax.ShapeDtypeStruct((s, s), jnp.int8),
        scratch_shapes=[pltpu.VMEM((tq, s), I32), pltpu.VMEM((tq, s), BF16),
                        pltpu.VMEM((IDX_HEADS, tq, LANES), F32)],
        compiler_params=_params(("parallel",)),
        name="indexer",
    )(zmisc, zmisc, ixk_t)


_NT = (((1,), (1,)), ((), ()))
FLASH_GROUP = 8


def _flash_kernel(qi_ref, ki_ref, *refs, kind, tb):
    if kind == "dsa":
        q_ref, k_ref, v_ref, mask_ref, b_ref, o_ref, m_sc, l_sc, acc_sc, mb_sc = refs
    else:
        q_ref, k_ref, kr_ref, v_ref, o_ref, m_sc, l_sc, acc_sc, mb_sc = refs
    p = pl.program_id(0)
    qi = qi_ref[p]
    ki = ki_ref[p]
    nsub = tb // LANES

    @pl.when(ki == 0)
    def _init():
        m_sc[...] = jnp.full(m_sc.shape, NEG_INF, F32)
        l_sc[...] = jnp.zeros(l_sc.shape, F32)
        acc_sc[...] = jnp.zeros(acc_sc.shape, F32)

    ones = jnp.ones((tb, LANES), BF16)

    def update(h, s):
        m_prev = m_sc[h]
        m_new = jnp.maximum(m_prev, jnp.max(s, axis=1, keepdims=True))
        alpha = jnp.exp2(m_prev - m_new)
        pr = jnp.exp2(s - _lane_tile(m_new, nsub)).astype(BF16)
        pv = jnp.dot(pr, jnp.concatenate([v_ref[h], ones], axis=1), preferred_element_type=F32)
        acc_sc[h] = alpha * acc_sc[h] + pv[:, :LANES]
        l_sc[h] = alpha * l_sc[h] + pv[:, LANES:]
        m_sc[h] = m_new

    def run_heads(logits):
        def group(g, carry):
            base = g * FLASH_GROUP
            s_next = logits(base)
            for j in range(FLASH_GROUP):
                s = s_next
                if j + 1 < FLASH_GROUP:
                    s_next = logits(base + j + 1)
                update(base + j, s)
            return carry

        lax.fori_loop(0, N_HEADS // FLASH_GROUP, group, 0)

    if kind == "dsa":
        mb_sc[...] = (mask_ref[...].astype(F32) - 1.0) * (-NEG_INF)

        def logits_far(h):
            return lax.dot_general(q_ref[h], k_ref[h], _NT, preferred_element_type=F32) + mb_sc[...]

        on_diag = ki == qi
        i_main = jnp.where(on_diag, 0, 2)
        i_sub = jnp.where(on_diag, 1, 2)
        i_corner = jnp.where(on_diag, 2, 1)

        def logits_near(h):
            s = logits_far(h)
            rows = []
            for a in range(nsub):
                blocks = []
                for b in range(nsub):
                    blk = s[a * LANES:(a + 1) * LANES, b * LANES:(b + 1) * LANES]
                    if b == a:
                        blk = blk + b_ref[i_main, h]
                    elif b == a - 1:
                        blk = blk + b_ref[i_sub, h]
                    elif a == 0 and b == nsub - 1:
                        blk = blk + b_ref[i_corner, h]
                    blocks.append(blk)
                rows.append(jnp.concatenate(blocks, axis=1))
            return jnp.concatenate(rows, axis=0)

        @pl.when(ki < qi - 1)
        def _far():
            run_heads(logits_far)

        @pl.when(ki >= qi - 1)
        def _near():
            run_heads(logits_near)
    else:
        @pl.when(p == 0)
        def _masks():
            rows = lax.broadcasted_iota(I32, (tb, tb), 0)
            cols = lax.broadcasted_iota(I32, (tb, tb), 1)
            vis = (cols >> CHUNK_SHIFT) <= (rows >> CHUNK_SHIFT)
            mb_sc[...] = jnp.where(vis, 0.0, NEG_INF)

        def qk(h):
            k = jnp.concatenate([k_ref[h], kr_ref[...]], axis=1)
            return lax.dot_general(q_ref[h], k, _NT, preferred_element_type=F32)

        @pl.when(ki < qi)
        def _off():
            run_heads(qk)

        @pl.when(ki == qi)
        def _diag():
            run_heads(lambda h: qk(h) + mb_sc[...])

    @pl.when(ki == qi)
    def _finish():
        for h in range(N_HEADS):
            o_ref[:, h * LANES:(h + 1) * LANES] = acc_sc[h] / l_sc[h]


def _flash(kind, q, k, v, extra, seq, tb=512):
    nq = seq // tb
    pairs = [(a, b) for a in range(nq) for b in range(a + 1)]
    qi = jnp.asarray([a for a, _ in pairs], I32)
    ki = jnp.asarray([b for _, b in pairs], I32)
    dq = q.shape[2]
    qspec = pl.BlockSpec((N_HEADS, tb, dq), lambda p, qi, ki: (0, qi[p], 0))
    scratch = [pltpu.VMEM((N_HEADS, tb, LANES), F32), pltpu.VMEM((N_HEADS, tb, LANES), F32),
               pltpu.VMEM((N_HEADS, tb, LANES), F32)]
    if kind == "dsa":
        mask, btiles = extra
        in_specs = [qspec,
                    pl.BlockSpec((N_HEADS, tb, LANES), lambda p, qi, ki: (0, ki[p], 0)),
                    pl.BlockSpec((N_HEADS, tb, LANES), lambda p, qi, ki: (0, ki[p], 0)),
                    pl.BlockSpec((tb, tb), lambda p, qi, ki: (qi[p], ki[p])),
                    pl.BlockSpec(btiles.shape, lambda p, qi, ki: (0, 0, 0, 0))]
        args = (q, k, v, mask, btiles)
        scratch = scratch + [pltpu.VMEM((tb, tb), F32)]
    else:
        (krp,) = extra
        in_specs = [qspec,
                    pl.BlockSpec((N_HEADS, tb, LANES), lambda p, qi, ki: (0, ki[p], 0)),
                    pl.BlockSpec((tb, LANES), lambda p, qi, ki: (ki[p], 0)),
                    pl.BlockSpec((N_HEADS, tb, LANES), lambda p, qi, ki: (1, ki[p], 0))]
        args = (q, k, krp, v)
        scratch = scratch + [pltpu.VMEM((tb, tb), F32)]
    return pl.pallas_call(
        functools.partial(_flash_kernel, kind=kind, tb=tb),
        grid_spec=pltpu.PrefetchScalarGridSpec(
            num_scalar_prefetch=2,
            grid=(len(pairs),),
            in_specs=in_specs,
            out_specs=pl.BlockSpec((tb, N_HEADS * LANES), lambda p, qi, ki: (qi[p], 0)),
            scratch_shapes=scratch),
        out_shape=jax.ShapeDtypeStruct((seq, N_HEADS * LANES), F32),
        compiler_params=_params(("arbitrary",)),
        name="flash_" + kind,
    )(qi, ki, *args)


def _sample_kernel(ixq_ref, wb_ref, ikT_ref, aq_ref, ks_ref, vs_ref, bias_ref, qc_ref, kvb_ref,
                   krp_ref, oa_ref, ob_ref, *, n_new, s_valid, past, topk):
    sp = ikT_ref.shape[2]
    qpos = past + lax.broadcasted_iota(I32, (n_new, sp), 0)
    kpos = lax.broadcasted_iota(I32, (n_new, sp), 1)
    adm = (kpos < s_valid) & ((kpos >> CHUNK_SHIFT) <= (qpos >> CHUNK_SHIFT))

    qb = ixq_ref[...].astype(BF16)
    w = wb_ref[...]
    kT = ikT_ref[0]
    off = MISC_IXW - MISC_KRSW
    score = jnp.zeros((n_new, sp), F32)
    for h in range(IDX_HEADS):
        d = jnp.dot(qb[:, h * IDX_DIM:(h + 1) * IDX_DIM], kT, preferred_element_type=F32)
        score = score + jnp.maximum(d, 0.0) * (w[:, off + h:off + h + 1] * (IDX_HEADS ** -0.5))
    key = jnp.where(adm, _sort_key(score), KEY_MASKED)

    kf = float(topk)

    def count_ge(cand):
        return jnp.sum(jnp.where(key >= cand, 1.0, 0.0), axis=1, keepdims=True)

    ans = jnp.where(count_ge(jnp.zeros((n_new, 1), I32)) >= kf, 0, INT_MIN).astype(I32)

    def bit_step(b, ans):
        cand = ans + jnp.left_shift(jnp.int32(1), 30 - b)
        return jnp.where(count_ge(cand) >= kf, cand, ans)

    ans = lax.fori_loop(0, 31, bit_step, ans)
    sel = key >= jnp.maximum(ans, KEY_HALF + 1)

    def attend(s, vmat):
        m = jnp.max(s, axis=1, keepdims=True)
        pr = jnp.exp2(s - m)
        l = jnp.sum(pr, axis=1, keepdims=True)
        return jnp.dot(pr.astype(BF16), vmat, preferred_element_type=F32) / l

    for h in range(N_HEADS):
        hs = slice(h * LANES, (h + 1) * LANES)
        s = lax.dot_general(aq_ref[h], ks_ref[0, :, hs], _NT, preferred_element_type=F32)
        s = jnp.where(sel, s + bias_ref[h], NEG_INF)
        oa_ref[:, hs] = attend(s, vs_ref[0, :, hs])

    krp = krp_ref[0]
    hd = N_HEADS * LANES
    for h in range(N_HEADS):
        hs = slice(h * LANES, (h + 1) * LANES)
        k = jnp.concatenate([kvb_ref[0, :, hs], krp], axis=1)
        s = lax.dot_general(qc_ref[h], k, _NT, preferred_element_type=F32)
        s = jnp.where(adm, s, NEG_INF)
        ob_ref[:, hs] = attend(s, kvb_ref[0, :, hd + h * LANES:hd + (h + 1) * LANES])


def _sample_attn(zmisc, ixk_t, aq_hm, ks, vs, bias, qcat, kvb, krp, n_new, s_valid, past, topk):
    nb, sp = ks.shape[0], ks.shape[1]
    hd = N_HEADS * LANES
    per_stream = lambda a: pl.BlockSpec((1,) + a.shape[1:], lambda b: (b,) + (0,) * (a.ndim - 1))
    return pl.pallas_call(
        functools.partial(_sample_kernel, n_new=n_new, s_valid=s_valid, past=past, topk=topk),
        grid=(nb,),
        in_specs=[pl.BlockSpec((n_new, IDX_HEADS * IDX_DIM), lambda b: (b, 0)),
                  pl.BlockSpec((n_new, LANES), lambda b: (b, MISC_KRSW // LANES)),
                  per_stream(ixk_t),
                  pl.BlockSpec((N_HEADS, n_new, LANES), lambda b: (0, b, 0)),
                  per_stream(ks), per_stream(vs),
                  pl.BlockSpec(bias.shape, lambda b: (0, 0, 0)),
                  pl.BlockSpec((N_HEADS, n_new, 256), lambda b: (0, b, 0)),
                  per_stream(kvb), per_stream(krp)],
        out_specs=[pl.BlockSpec((n_new, hd), lambda b: (b, 0)),
                   pl.BlockSpec((n_new, hd), lambda b: (b, 0))],
        out_shape=[jax.ShapeDtypeStruct((nb * n_new, hd), F32),
                   jax.ShapeDtypeStruct((nb * n_new, hd), F32)],
        compiler_params=_params(("parallel",)),
        name="sample_attn",
    )(zmisc, zmisc, ixk_t, aq_hm, ks, vs, bias, qcat, kvb, krp)


def _outproj_kernel(x_ref, ga_ref, gb_ref, oa_ref, ob_ref, w_ref, o_ref):
    mix = ga_ref[...] * oa_ref[...] + gb_ref[...] * ob_ref[...]
    o_ref[...] = x_ref[...] + jnp.dot(mix.astype(BF16), w_ref[...], preferred_element_type=F32)


def _outproj(x, ga, gb, oa, ob, w, tm):
    m, d = x.shape
    row = pl.BlockSpec((tm, d), lambda i: (i, 0))
    return pl.pallas_call(
        _outproj_kernel,
        grid=(m // tm,),
        in_specs=[row, row, row, row, row, pl.BlockSpec(w.shape, lambda i: (0, 0))],
        out_specs=row,
        out_shape=jax.ShapeDtypeStruct((m, d), F32),
        compiler_params=_params(("parallel",)),
        name="outproj",
    )(x, ga, gb, oa, ob, w)


def _ffn_kernel(x_ref, g_ref, wu_ref, wd_ref, gf_ref, o_ref, h_sc, acc_sc):
    f = pl.program_id(1)

    @pl.when(f == 0)
    def _start():
        h_sc[...] = _rms(x_ref[...], g_ref[...]).astype(BF16)
        acc_sc[...] = jnp.zeros(acc_sc.shape, F32)

    u = jnp.dot(h_sc[...], wu_ref[...], preferred_element_type=F32)
    u = jnp.square(jnp.maximum(u, 0.0)).astype(BF16)
    acc_sc[...] += jnp.dot(u, wd_ref[...], preferred_element_type=F32)

    @pl.when(f == pl.num_programs(1) - 1)
    def _end():
        o_ref[...] = _rms(x_ref[...] + acc_sc[...], gf_ref[...])


def _ffn(x, g, wu, wd, gf, tm, tf=512):
    m, d = x.shape
    dff = wu.shape[1]
    return pl.pallas_call(
        _ffn_kernel,
        grid=(m // tm, dff // tf),
        in_specs=[pl.BlockSpec((tm, d), lambda i, f: (i, 0)),
                  pl.BlockSpec((1, d), lambda i, f: (0, 0)),
                  pl.BlockSpec((d, tf), lambda i, f: (0, f)),
                  pl.BlockSpec((tf, d), lambda i, f: (f, 0)),
                  pl.BlockSpec((1, d), lambda i, f: (0, 0))],
        out_specs=pl.BlockSpec((tm, d), lambda i, f: (i, 0)),
        out_shape=jax.ShapeDtypeStruct((m, d), F32),
        scratch_shapes=[pltpu.VMEM((tm, d), BF16), pltpu.VMEM((tm, d), F32)],
        compiler_params=_params(("parallel", "arbitrary")),
        name="ffn",
    )(x, g.reshape(1, d), wu, wd, gf.reshape(1, d))


def _t5_bucket(rel):
    nb = REL_BUCKETS // 2
    ret = (rel > 0).astype(jnp.int32) * nb
    n = jnp.abs(rel)
    max_exact = nb // 2
    large = max_exact + (jnp.log(jnp.maximum(n, 1).astype(jnp.float32) / max_exact)
                         / math.log(REL_MAX_DIST / max_exact) * (nb - max_exact)).astype(jnp.int32)
    large = jnp.minimum(large, nb - 1)
    return ret + jnp.where(n < max_exact, n, large)


def _rel_bias(table, rel):
    onehot = jax.nn.one_hot(_t5_bucket(rel), REL_BUCKETS, dtype=F32)
    return jnp.einsum("...b,bh->...h", onehot, table.astype(F32), precision=lax.Precision.HIGHEST)


def _rope_tables(pos):
    half = ROPE_DIM // 2
    inv_freq = jnp.power(ROPE_THETA, -jnp.arange(half, dtype=jnp.float32) / half)
    ang = pos.astype(jnp.float32)[:, None] * inv_freq[None, :]
    cos, sin = jnp.cos(ang), jnp.sin(ang)
    pad = jnp.zeros((pos.shape[0], LANES - ROPE_DIM), F32)
    return (jnp.concatenate([cos, cos, pad], axis=1), jnp.concatenate([-sin, sin, pad], axis=1))


def _swap_halves(w):
    half = ROPE_DIM // 2
    return jnp.concatenate([w[..., half:], w[..., :half]], axis=-1)


def _prep_weights(w_in, w_uq, w_uk, w_uv, w_out, w_ff_up, w_ff_down):
    d = w_in.shape[0]
    hd = N_HEADS * HEAD_DIM
    sizes = (hd, hd, hd, IDX_HEADS * IDX_DIM, IDX_DIM, IDX_HEADS, Q_LORA, KV_LORA, ROPE_DIM, d, d)
    cols, off = [], 0
    for n in sizes:
        cols.append(w_in[:, off:off + n])
        off += n
    wq, wk, wv, wixq, wixk, wixw, wcq, wckv, wkr, wga, wgb = cols
    pad = jnp.zeros((d, MISC_COLS - MISC_IXW - IDX_HEADS), w_in.dtype)
    wmisc = jnp.concatenate([wixq, wcq, wckv, wkr, wixk, _swap_halves(wkr), wixw, pad], axis=1)
    nope, rope = w_uq[..., :QK_NOPE_DIM], w_uq[..., QK_NOPE_DIM:]
    zpad = jnp.zeros(rope.shape, w_uq.dtype)
    wq_cat = jnp.concatenate([nope, rope, zpad], axis=-1).reshape(Q_LORA, N_HEADS * 256)
    wq_sw = jnp.concatenate([_swap_halves(rope), zpad], axis=-1).reshape(Q_LORA, N_HEADS * LANES)
    wkv = jnp.concatenate([w_uk.reshape(KV_LORA, hd), w_uv.reshape(KV_LORA, hd)], axis=1)
    c = lambda a: a.astype(BF16)
    return dict(q=c(wq), k=c(wk), v=c(wv), misc=c(wmisc), ga=c(wga), gb=c(wgb), wq_cat=c(wq_cat),
                wq_sw=c(wq_sw), wkv=c(wkv), out=c(w_out), up=c(w_ff_up), down=c(w_ff_down))


def _front(x, pos, w, g_mix, g_q, g_kv, tm):
    h = _norm_cast(x, g_mix, tm)
    aq_hm = _mm(h, w["q"], "heads", tm, head_scale=A_SCALE * LOG2E)
    ak, ak_hm = _mm(h, w["k"], "f32+heads", tm)
    av, av_hm = _mm(h, w["v"], "f32+heads", tm)
    zmisc = _mm(h, w["misc"], "f32", tm)
    sga = _mm(h, w["ga"], "sigmoid", tm)
    sgb = _mm(h, w["gb"], "sigmoid", tm)
    cos_t, sin_t = _rope_tables(pos)
    qcat, ckv, ckv_b, kr, krp = _mla_prep(zmisc, cos_t, sin_t, g_q, g_kv, w["wq_cat"], w["wq_sw"], tm)
    return dict(aq_hm=aq_hm, ak=ak, ak_hm=ak_hm, av=av, av_hm=av_hm, zmisc=zmisc, sga=sga, sgb=sgb,
                qcat=qcat, ckv=ckv, ckv_b=ckv_b, kr=kr, krp=krp)


def _back(x, f, oa, ob, w, g_ffn, g_final, tm_out, tm_ffn):
    x1 = _outproj(x, f["sga"], f["sgb"], oa, ob, w["out"], tm_out)
    return _ffn(x1, g_ffn, w["up"], w["down"], g_final, tm_ffn)


def kernel(x_prompt, x_sample, cache_a_k, cache_a_v, cache_a_idx_k, cache_b_ckv, cache_b_krope,
           rel_bias_table, norm_mix_g, w_in, q_lora_g, w_uq, kv_lora_g, w_uk, w_uv, w_out,
           norm_ffn_g, w_ff_up, w_ff_down, final_norm_g):
    assert w_in.shape[0] == 1, "single-layer trunk"
    _, seq, d = x_prompt.shape
    nb, n_new, _ = x_sample.shape
    past = cache_a_k.shape[2]
    s_valid = past + n_new
    sp = -(-s_valid // LANES) * LANES
    topk_p = min(TOPK_MAX, seq // 4)
    topk_s = min(TOPK_MAX, s_valid // 4)
    hd = N_HEADS * HEAD_DIM

    w = _prep_weights(w_in[0], w_uq[0], w_uk[0], w_uv[0], w_out[0], w_ff_up[0], w_ff_down[0])

    xp = x_prompt.reshape(seq, d)
    fp = _front(xp, jnp.arange(seq, dtype=jnp.int32), w, norm_mix_g[0], q_lora_g[0], kv_lora_g[0], 512)
    ixk_p = fp["zmisc"][:, MISC_IXK:MISC_IXK + IDX_DIM]
    mask = _indexer(fp["zmisc"], ixk_p.astype(BF16).T, topk_p)
    ii = jnp.arange(LANES, dtype=jnp.int32)
    rel0 = ii[None, :] - ii[:, None]
    far = _rel_bias(rel_bias_table, jnp.full((1, 1), -REL_MAX_DIST, jnp.int32))
    btiles = jnp.stack([_rel_bias(rel_bias_table, rel0) - far,
                        _rel_bias(rel_bias_table, rel0 - LANES) - far,
                        jnp.zeros((LANES, LANES, N_HEADS), F32)])
    btiles = jnp.transpose(btiles, (0, 3, 1, 2)).astype(F32) * LOG2E
    oa_p = _flash("dsa", fp["aq_hm"], fp["ak_hm"], fp["av_hm"], (mask, btiles), seq)
    kvb_p = _mm(fp["ckv_b"], w["wkv"], "heads", 512)
    ob_p = _flash("mla", fp["qcat"], kvb_p, kvb_p, (fp["krp"],), seq)
    y_p = _back(xp, fp, oa_p, ob_p, w, norm_ffn_g[0], final_norm_g, 256, 512)

    m_s = nb * n_new
    xs = x_sample.reshape(m_s, d)
    pos_s = jnp.tile(past + jnp.arange(n_new, dtype=jnp.int32), nb)
    fs = _front(xs, pos_s, w, norm_mix_g[0], q_lora_g[0], kv_lora_g[0], m_s)
    ixk_s = fs["zmisc"][:, MISC_IXK:MISC_IXK + IDX_DIM]

    def with_cache(cache, new, width):
        a = jnp.concatenate([cache.reshape(nb, past, width).astype(BF16),
                             new.reshape(nb, n_new, width).astype(BF16)], axis=1)
        return jnp.pad(a, ((0, 0), (0, sp - s_valid), (0, 0)))

    ks = with_cache(cache_a_k[0], fs["ak"], hd)
    vs = with_cache(cache_a_v[0], fs["av"], hd)
    ixk_all = jnp.swapaxes(with_cache(cache_a_idx_k[0], ixk_s, IDX_DIM), 1, 2)
    ckv_all = with_cache(cache_b_ckv[0], fs["ckv"], KV_LORA)
    krope_pad = jnp.pad(cache_b_krope[0], ((0, 0), (0, 0), (0, LANES - ROPE_DIM)))
    krp_all = with_cache(krope_pad, fs["krp"], LANES)
    kvb_s = _mm(ckv_all.reshape(nb * sp, KV_LORA), w["wkv"], "bf16", sp).reshape(nb, sp, 2 * hd)
    qpos = past + jnp.arange(n_new, dtype=jnp.int32)
    kpos = jnp.arange(sp, dtype=jnp.int32)
    bias_s = jnp.transpose(_rel_bias(rel_bias_table, kpos[None, :] - qpos[:, None]), (2, 0, 1))
    oa_s, ob_s = _sample_attn(fs["zmisc"], ixk_all, fs["aq_hm"], ks, vs, bias_s.astype(F32) * LOG2E,
                              fs["qcat"], kvb_s, krp_all, n_new, s_valid, past, topk_s)
    y_s = _back(xs, fs, oa_s, ob_s, w, norm_ffn_g[0], final_norm_g, m_s, m_s)

    st = lambda a, b, *tail: a.reshape((1, b, -1) + tail)
    return (y_p.reshape(1, seq, d), y_s.reshape(nb, n_new, d),
            st(fp["ak"], 1, N_HEADS, HEAD_DIM), st(fp["av"], 1, N_HEADS, HEAD_DIM),
            st(ixk_p, 1, IDX_DIM), st(fp["ckv"], 1, KV_LORA), st(fp["kr"], 1, ROPE_DIM),
            st(fs["ak"], nb, N_HEADS, HEAD_DIM), st(fs["av"], nb, N_HEADS, HEAD_DIM),
            st(ixk_s, nb, IDX_DIM), st(fs["ckv"], nb, KV_LORA), st(fs["kr"], nb, ROPE_DIM))
```

```python
import functools
import math

import numpy as np
import jax
import jax.numpy as jnp
from jax import lax
from jax.experimental import pallas as pl
from jax.experimental.pallas import tpu as pltpu

F32 = jnp.float32
BF16 = jnp.bfloat16
I32 = jnp.int32

CHUNK = 64
CHUNK_SHIFT = 6
N_HEADS = 16
HEAD_DIM = 128
IDX_HEADS = 16
IDX_DIM = 64
TOPK_MAX = 256
REL_BUCKETS = 32
REL_MAX_DIST = 128
QK_NOPE_DIM = 128
ROPE_DIM = 64
Q_LORA = 512
KV_LORA = 256
ROPE_THETA = 10000.0
MLA_SCALE = (QK_NOPE_DIM + ROPE_DIM) ** -0.5
A_SCALE = HEAD_DIM ** -0.5
EPS = 1e-6
NEG_INF = -1e30
LOG2E = math.log2(math.e)

LANES = 128
V7X_VMEM_LIMIT = 56 * 1024 * 1024

MISC_IXQ = 0
MISC_CQ = 1024
MISC_CKV = 1536
MISC_KR = 1792
MISC_IXK = 1856
MISC_KRSW = 1920
MISC_IXW = 1984
MISC_COLS = 2048


def _sortable_key_const(v):
    bits = int(np.float32(v).view(np.int32))
    return bits ^ ((bits >> 31) & 0x7FFFFFFF)


KEY_MASKED = _sortable_key_const(NEG_INF)
KEY_HALF = _sortable_key_const(np.float32(0.5) * np.float32(NEG_INF))
INT_MIN = -(2 ** 31)
HI16_MASK = -65536


def _params(sem):
    return pltpu.CompilerParams(dimension_semantics=sem, vmem_limit_bytes=V7X_VMEM_LIMIT)


def _rms(x, g):
    ms = jnp.mean(x * x, axis=-1, keepdims=True)
    return (x * lax.rsqrt(ms + EPS)) * g


def _sort_key(x):
    bits = lax.bitcast_convert_type(x, I32)
    return bits ^ ((bits >> 31) & 0x7FFFFFFF)


def _lane_tile(x, n):
    return x if n == 1 else jnp.concatenate([x] * n, axis=1)


def _norm_cast_kernel(x_ref, g_ref, o_ref):
    o_ref[...] = _rms(x_ref[...], g_ref[...]).astype(o_ref.dtype)


def _norm_cast(x, g, tm):
    m, d = x.shape
    return pl.pallas_call(
        _norm_cast_kernel,
        grid=(m // tm,),
        in_specs=[pl.BlockSpec((tm, d), lambda i: (i, 0)),
                  pl.BlockSpec((1, d), lambda i: (0, 0))],
        out_specs=pl.BlockSpec((tm, d), lambda i: (i, 0)),
        out_shape=jax.ShapeDtypeStruct((m, d), BF16),
        compiler_params=_params(("parallel",)),
        name="norm_cast",
    )(x, g.reshape(1, d))


def _mm_kernel(x_ref, w_ref, *o_refs, mode, head_scale):
    acc = jnp.dot(x_ref[...], w_ref[...], preferred_element_type=F32)
    if mode == "f32":
        o_refs[0][...] = acc
    elif mode == "sigmoid":
        o_refs[0][...] = jax.nn.sigmoid(acc)
    elif mode == "bf16":
        o_refs[0][...] = acc.astype(BF16)
    else:
        if mode == "f32+heads":
            o_refs[0][...] = acc
        hm = o_refs[-1]
        for h in range(hm.shape[0]):
            blk = acc[:, h * LANES:(h + 1) * LANES]
            hm[h] = (blk if head_scale == 1.0 else blk * head_scale).astype(BF16)


def _mm(x, w, mode, tm, head_scale=1.0):
    m, k = x.shape
    n = w.shape[1]
    flat = pl.BlockSpec((tm, n), lambda i: (i, 0))
    heads = pl.BlockSpec((n // LANES, tm, LANES), lambda i: (0, i, 0))
    hm_shape = jax.ShapeDtypeStruct((n // LANES, m, LANES), BF16)
    if mode in ("f32", "sigmoid"):
        out_specs, out_shape = flat, jax.ShapeDtypeStruct((m, n), F32)
    elif mode == "bf16":
        out_specs, out_shape = flat, jax.ShapeDtypeStruct((m, n), BF16)
    elif mode == "heads":
        out_specs, out_shape = heads, hm_shape
    else:
        out_specs = [flat, heads]
        out_shape = [jax.ShapeDtypeStruct((m, n), F32), hm_shape]
    return pl.pallas_call(
        functools.partial(_mm_kernel, mode=mode, head_scale=head_scale),
        grid=(m // tm,),
        in_specs=[pl.BlockSpec((tm, k), lambda i: (i, 0)),
                  pl.BlockSpec((k, n), lambda i: (0, 0))],
        out_specs=out_specs,
        out_shape=out_shape,
        compiler_params=_params(("parallel",)),
        name="mm_" + mode.replace("+", "_"),
    )(x, w)


def _mla_prep_kernel(cq_ref, ckv_ref, a_ref, b_ref, cos_ref, sin_ref, gq_ref, gkv_ref,
                     wq_ref, wsw_ref, qcat_ref, ckv_o, ckvb_o, kr_o, krp_o):
    cq = _rms(cq_ref[...], gq_ref[...]).astype(BF16)
    ckv = _rms(ckv_ref[...], gkv_ref[...])
    ckv_o[...] = ckv
    ckvb_o[...] = ckv.astype(BF16)
    cos = cos_ref[...]
    sin = sin_ref[...]
    r = a_ref[...] * cos + b_ref[...] * sin
    kr_o[...] = r[:, :ROPE_DIM]
    krp_o[...] = r.astype(BF16)
    for h in range(N_HEADS):
        qh = jnp.dot(cq, wq_ref[:, h * 256:(h + 1) * 256], preferred_element_type=F32)
        qs = jnp.dot(cq, wsw_ref[:, h * LANES:(h + 1) * LANES], preferred_element_type=F32)
        rp = qh[:, LANES:] * cos + qs * sin
        qcat = jnp.concatenate([qh[:, :LANES], rp], axis=1) * (MLA_SCALE * LOG2E)
        qcat_ref[h] = qcat.astype(BF16)


def _mla_prep(zmisc, cos_t, sin_t, gq, gkv, wq_cat, wq_sw, tm):
    m = zmisc.shape[0]
    row = lambda w, j: pl.BlockSpec((tm, w), lambda i, j=j: (i, j))
    const = lambda a: pl.BlockSpec(a.shape, lambda i: (0,) * a.ndim)
    gq2, gkv2 = gq.reshape(1, Q_LORA), gkv.reshape(1, KV_LORA)
    return pl.pallas_call(
        _mla_prep_kernel,
        grid=(m // tm,),
        in_specs=[row(Q_LORA, MISC_CQ // Q_LORA), row(KV_LORA, MISC_CKV // KV_LORA),
                  row(LANES, MISC_KR // LANES), row(LANES, MISC_KRSW // LANES),
                  row(LANES, 0), row(LANES, 0), const(gq2), const(gkv2),
                  const(wq_cat), const(wq_sw)],
        out_specs=[pl.BlockSpec((N_HEADS, tm, 256), lambda i: (0, i, 0)),
                   row(KV_LORA, 0), row(KV_LORA, 0), row(ROPE_DIM, 0), row(LANES, 0)],
        out_shape=[jax.ShapeDtypeStruct((N_HEADS, m, 256), BF16),
                   jax.ShapeDtypeStruct((m, KV_LORA), F32),
                   jax.ShapeDtypeStruct((m, KV_LORA), BF16),
                   jax.ShapeDtypeStruct((m, ROPE_DIM), F32),
                   jax.ShapeDtypeStruct((m, LANES), BF16)],
        compiler_params=_params(("parallel",)),
        name="mla_prep",
    )(zmisc, zmisc, zmisc, zmisc, cos_t, sin_t, gq2, gkv2, wq_cat, wq_sw)


SCORE_TILE = 256


def _indexer_kernel(q_ref, wb_ref, kT_ref, mask_ref, key_sc, co_sc, w_sc, q_sc, *, tq, ts, cw, topk):
    i = pl.program_id(0)
    s_total = kT_ref.shape[2]
    t0 = i * tq
    front = t0 + tq
    nblk = (front + ts - 1) // ts
    nch = nblk * (ts // cw)

    w = wb_ref[...]
    off = MISC_IXW - MISC_KRSW
    for h in range(IDX_HEADS):
        w_sc[h] = jnp.broadcast_to(w[:, off + h:off + h + 1] * (IDX_HEADS ** -0.5), (tq, LANES))
    qb = q_ref[...].astype(BF16)
    npair = IDX_HEADS // 2
    for j in range(npair):
        q_sc[j * tq:(j + 1) * tq, :] = qb[:, j * LANES:(j + 1) * LANES]

    def score_block(kb, carry):
        c0 = pl.multiple_of(kb * ts, ts)
        for c in range(ts // SCORE_TILE):
            cc = pl.multiple_of(c0 + c * SCORE_TILE, SCORE_TILE)
            q_all = q_sc[...]
            d_par = [jnp.dot(q_all, kT_ref[par, :, pl.ds(cc, SCORE_TILE)], preferred_element_type=F32)
                     for par in range(2)]
            acc = jnp.zeros((tq, SCORE_TILE), F32)
            for h in range(IDX_HEADS):
                d = d_par[h % 2][(h // 2) * tq:(h // 2 + 1) * tq]
                acc = acc + jnp.maximum(d, 0.0) * _lane_tile(w_sc[h], SCORE_TILE // LANES)
            rows = t0 + lax.broadcasted_iota(I32, (tq, SCORE_TILE), 0)
            cols = cc + lax.broadcasted_iota(I32, (tq, SCORE_TILE), 1)
            adm = (cols >> CHUNK_SHIFT) <= (rows >> CHUNK_SHIFT)
            sc = jnp.where(adm, acc, NEG_INF)
            bits = lax.bitcast_convert_type(sc, I32)
            key_sc[:, pl.ds(cc, SCORE_TILE)] = bits ^ ((bits >> 31) & 0x7FFFFFFF)
            hi = lax.bitcast_convert_type(bits & HI16_MASK, F32)
            co_sc[:, pl.ds(cc, SCORE_TILE)] = hi.astype(BF16)
        return carry

    lax.fori_loop(0, nblk, score_block, 0)

    kf = float(topk)

    def count_ge(cand):
        def body(c, cnt):
            c0 = pl.multiple_of(c * cw, cw)
            k = key_sc[:, pl.ds(c0, cw)]
            for j in range(cw // LANES):
                cnt = cnt + jnp.where(k[:, j * LANES:(j + 1) * LANES] >= cand, 1.0, 0.0)
            return cnt
        cnt = lax.fori_loop(0, nch, body, jnp.zeros((tq, LANES), F32))
        return jnp.broadcast_to(jnp.sum(cnt, axis=1, keepdims=True), (tq, LANES))

    def pattern16(c16):
        return jnp.where(c16 >= 0, c16, c16 ^ 0x7FFF) & 0xFFFF

    def count_ge_coarse(c16):
        cand = lax.bitcast_convert_type(pattern16(c16) << 16, F32).astype(BF16)

        def body(c, cnt):
            c0 = pl.multiple_of(c * cw, cw)
            x = co_sc[:, pl.ds(c0, cw)]
            for j in range(cw // LANES):
                hit = x[:, j * LANES:(j + 1) * LANES] >= cand
                cnt = cnt + jnp.where(hit, jnp.ones((), BF16), jnp.zeros((), BF16))
            return cnt
        cnt = lax.fori_loop(0, nch, body, jnp.zeros((tq, LANES), BF16))
        total = jnp.sum(cnt.astype(F32), axis=1, keepdims=True)
        return jnp.broadcast_to(total, (tq, LANES))

    c0cnt = count_ge_coarse(jnp.zeros((tq, LANES), I32))
    pos = c0cnt >= kf
    t16 = jnp.where(pos, 0, -32768).astype(I32)
    cnt_lo = jnp.where(pos, c0cnt, float(s_total))

    def coarse_step(b, st):
        t16, cnt_lo = st
        cand = t16 + jnp.left_shift(jnp.int32(1), 14 - b)
        cnt = count_ge_coarse(cand)
        ok = cnt >= kf
        return jnp.where(ok, cand, t16), jnp.where(ok, cnt, cnt_lo)

    t16, cnt_lo = lax.fori_loop(0, 15, coarse_step, (t16, cnt_lo))

    pat = pattern16(t16)
    b0 = pat << 16
    b1 = b0 | 0xFFFF
    k0 = b0 ^ ((b0 >> 31) & 0x7FFFFFFF)
    k1 = b1 ^ ((b1 >> 31) & 0x7FFFFFFF)
    is_zero = (pat & 0x7FFF) == 0
    lo = jnp.where(is_zero, -65536, jnp.minimum(k0, k1))
    hi = jnp.where(is_zero, 65535, jnp.maximum(k0, k1))
    rows1 = t0 + lax.broadcasted_iota(I32, (tq, LANES), 0)
    few = (((rows1 >> CHUNK_SHIFT) + 1) << CHUNK_SHIFT) < topk

    def active_of(lo, hi, cnt_lo):
        done = few | (cnt_lo == kf) | (lo >= hi)
        return jnp.max(jnp.where(done, 0.0, 1.0))

    def fine_cond(st):
        return st[3] > 0.5

    def fine_step(st):
        lo, hi, cnt_lo, _ = st
        done = few | (cnt_lo == kf) | (lo >= hi)
        mid = lo + ((hi - lo + 1) >> 1)
        cnt = count_ge(mid)
        up = (cnt >= kf) & jnp.logical_not(done)
        down = (cnt < kf) & jnp.logical_not(done)
        lo = jnp.where(up, mid, lo)
        cnt_lo = jnp.where(up, cnt, cnt_lo)
        hi = jnp.where(down, mid - 1, hi)
        return lo, hi, cnt_lo, active_of(lo, hi, cnt_lo)

    lo, hi, cnt_lo, _ = lax.while_loop(fine_cond, fine_step, (lo, hi, cnt_lo, active_of(lo, hi, cnt_lo)))
    ans = jnp.where(few, INT_MIN, lo)
    thr = _lane_tile(jnp.maximum(ans, KEY_HALF + 1), cw // LANES)
    tied = jnp.logical_not(few) & (cnt_lo > kf) & (lo > KEY_HALF)
    any_tied = jnp.max(jnp.where(tied, 1.0, 0.0))

    def write_mask(c, carry):
        c0 = pl.multiple_of(c * cw, cw)
        sel = key_sc[:, pl.ds(c0, cw)] >= thr
        mask_ref[:, pl.ds(c0, cw)] = jnp.where(sel, 1, 0).astype(jnp.int8)
        return carry

    @pl.when(any_tied < 0.5)
    def _plain():
        lax.fori_loop(0, nch, write_mask, 0)

    @pl.when(any_tied >= 0.5)
    def _with_ties():
        need = kf - count_ge(lo + 1)
        lo_t = _lane_tile(lo, cw // LANES)
        tied_t = _lane_tile(jnp.where(tied, 1.0, 0.0), cw // LANES) > 0.5
        need_t = _lane_tile(need, cw // LANES)
        jj = lax.broadcasted_iota(I32, (cw, cw), 0)
        ii = lax.broadcasted_iota(I32, (cw, cw), 1)
        upper = jnp.where(jj <= ii, 1.0, 0.0).astype(BF16)

        def write_tied(c, seen):
            c0 = pl.multiple_of(c * cw, cw)
            k = key_sc[:, pl.ds(c0, cw)]
            eq = jnp.where(k == lo_t, 1.0, 0.0)
            incl = jnp.dot(eq.astype(BF16), upper, preferred_element_type=F32)
            rank = _lane_tile(seen, cw // LANES) + incl - eq
            take = (k > lo_t) | ((eq > 0.5) & (rank < need_t))
            sel = (tied_t & take) | (jnp.logical_not(tied_t) & (k >= thr))
            mask_ref[:, pl.ds(c0, cw)] = jnp.where(sel, 1, 0).astype(jnp.int8)
            return seen + jnp.broadcast_to(incl[:, cw - 1:cw], (tq, LANES))

        lax.fori_loop(0, nch, write_tied, jnp.zeros((tq, LANES), F32))

    def zero_mask(c, carry):
        c0 = pl.multiple_of(c * cw, cw)
        mask_ref[:, pl.ds(c0, cw)] = jnp.zeros((tq, cw), jnp.int8)
        return carry

    lax.fori_loop(nch, s_total // cw, zero_mask, 0)


def _indexer(zmisc, ixk_t, topk, tq=128, ts=1024, cw=512):
    s = ixk_t.shape[1]
    assert ts % cw == 0 and s % ts == 0 and ts % tq == 0
    zero = jnp.zeros_like(ixk_t)
    ixk_t = jnp.stack([jnp.concatenate([ixk_t, zero], axis=0), jnp.concatenate([zero, ixk_t], axis=0)])
    return pl.pallas_call(
        functools.partial(_indexer_kernel, tq=tq, ts=ts, cw=cw, topk=topk),
        grid=(s // tq,),
        in_specs=[pl.BlockSpec((tq, IDX_HEADS * IDX_DIM), lambda i: (i, 0)),
                  pl.BlockSpec((tq, LANES), lambda i: (i, MISC_KRSW // LANES)),
                  pl.BlockSpec((2, 2 * IDX_DIM, s), lambda i: (0, 0, 0))],
        out_specs=pl.BlockSpec((tq, s), lambda i: (i, 0)),
        out_shape=jax.ShapeDtypeStruct((s, s), jnp.int8),
        scratch_shapes=[pltpu.VMEM((tq, s), I32), pltpu.VMEM((tq, s), BF16),
                        pltpu.VMEM((IDX_HEADS, tq, LANES), F32),
                        pltpu.VMEM((IDX_HEADS // 2 * tq, 2 * IDX_DIM), BF16)],
        compiler_params=_params(("parallel",)),
        name="indexer",
    )(zmisc, zmisc, ixk_t)


_NT = (((1,), (1,)), ((), ()))
FLASH_AHEAD = 2
FLASH_GROUP = 16


def _flash_kernel(qi_ref, ki_ref, *refs, kind, tb):
    if kind == "dsa":
        q_ref, k_ref, v_ref, mask_ref, b_ref, o_ref, m_sc, l_sc, acc_sc, mb_sc = refs
    else:
        q_ref, k_ref, kr_ref, v_ref, o_ref, m_sc, l_sc, acc_sc, mb_sc = refs
    p = pl.program_id(0)
    qi = qi_ref[p]
    ki = ki_ref[p]
    nsub = tb // LANES

    @pl.when(ki == 0)
    def _init():
        m_sc[...] = jnp.full(m_sc.shape, NEG_INF, F32)
        l_sc[...] = jnp.zeros(l_sc.shape, F32)
        acc_sc[...] = jnp.zeros(acc_sc.shape, F32)

    ones = jnp.ones((tb, LANES), BF16)

    def update(h, s):
        m_prev = m_sc[h]
        m_new = jnp.maximum(m_prev, jnp.max(s, axis=1, keepdims=True))
        alpha = jnp.exp2(m_prev - m_new)
        pr = jnp.exp2(s - _lane_tile(m_new, nsub)).astype(BF16)
        pv = jnp.dot(pr, jnp.concatenate([v_ref[h], ones], axis=1), preferred_element_type=F32)
        acc_sc[h] = alpha * acc_sc[h] + pv[:, :LANES]
        l_sc[h] = alpha * l_sc[h] + pv[:, LANES:]
        m_sc[h] = m_new

    def run_heads(logits):
        def group(g, carry):
            base = g * FLASH_GROUP
            ahead = [logits(base + j) for j in range(min(FLASH_AHEAD, FLASH_GROUP))]
            for j in range(FLASH_GROUP):
                s = ahead.pop(0)
                if j + FLASH_AHEAD < FLASH_GROUP:
                    ahead.append(logits(base + j + FLASH_AHEAD))
                update(base + j, s)
            return carry

        lax.fori_loop(0, N_HEADS // FLASH_GROUP, group, 0)

    if kind == "dsa":
        mb_sc[...] = (mask_ref[...].astype(F32) - 1.0) * (-NEG_INF)

        def logits_far(h):
            return lax.dot_general(q_ref[h], k_ref[h], _NT, preferred_element_type=F32) + mb_sc[...]

        on_diag = ki == qi
        i_main = jnp.where(on_diag, 0, 2)
        i_sub = jnp.where(on_diag, 1, 2)
        i_corner = jnp.where(on_diag, 2, 1)

        def logits_near(h):
            s = logits_far(h)
            rows = []
            for a in range(nsub):
                blocks = []
                for b in range(nsub):
                    blk = s[a * LANES:(a + 1) * LANES, b * LANES:(b + 1) * LANES]
                    if b == a:
                        blk = blk + b_ref[i_main, h]
                    elif b == a - 1:
                        blk = blk + b_ref[i_sub, h]
                    elif a == 0 and b == nsub - 1:
                        blk = blk + b_ref[i_corner, h]
                    blocks.append(blk)
                rows.append(jnp.concatenate(blocks, axis=1))
            return jnp.concatenate(rows, axis=0)

        @pl.when(ki < qi - 1)
        def _far():
            run_heads(logits_far)

        @pl.when(ki >= qi - 1)
        def _near():
            run_heads(logits_near)
    else:
        @pl.when(p == 0)
        def _masks():
            rows = lax.broadcasted_iota(I32, (tb, tb), 0)
            cols = lax.broadcasted_iota(I32, (tb, tb), 1)
            vis = (cols >> CHUNK_SHIFT) <= (rows >> CHUNK_SHIFT)
            mb_sc[...] = jnp.where(vis, 0.0, NEG_INF)

        def qk(h):
            k = jnp.concatenate([k_ref[h], kr_ref[...]], axis=1)
            return lax.dot_general(q_ref[h], k, _NT, preferred_element_type=F32)

        @pl.when(ki < qi)
        def _off():
            run_heads(qk)

        @pl.when(ki == qi)
        def _diag():
            run_heads(lambda h: qk(h) + mb_sc[...])

    @pl.when(ki == qi)
    def _finish():
        for h in range(N_HEADS):
            o_ref[:, h * LANES:(h + 1) * LANES] = acc_sc[h] / l_sc[h]


def _flash(kind, q, k, v, extra, seq, tb=512):
    nq = seq // tb
    pairs = [(a, b) for a in range(nq) for b in range(a + 1)]
    qi = jnp.asarray([a for a, _ in pairs], I32)
    ki = jnp.asarray([b for _, b in pairs], I32)
    dq = q.shape[2]
    qspec = pl.BlockSpec((N_HEADS, tb, dq), lambda p, qi, ki: (0, qi[p], 0))
    scratch = [pltpu.VMEM((N_HEADS, tb, LANES), F32), pltpu.VMEM((N_HEADS, tb, LANES), F32),
               pltpu.VMEM((N_HEADS, tb, LANES), F32)]
    if kind == "dsa":
        mask, btiles = extra
        in_specs = [qspec,
                    pl.BlockSpec((N_HEADS, tb, LANES), lambda p, qi, ki: (0, ki[p], 0)),
                    pl.BlockSpec((N_HEADS, tb, LANES), lambda p, qi, ki: (0, ki[p], 0)),
                    pl.BlockSpec((tb, tb), lambda p, qi, ki: (qi[p], ki[p])),
                    pl.BlockSpec(btiles.shape, lambda p, qi, ki: (0, 0, 0, 0))]
        args = (q, k, v, mask, btiles)
        scratch = scratch + [pltpu.VMEM((tb, tb), F32)]
    else:
        (krp,) = extra
        in_specs = [qspec,
                    pl.BlockSpec((N_HEADS, tb, LANES), lambda p, qi, ki: (0, ki[p], 0)),
                    pl.BlockSpec((tb, LANES), lambda p, qi, ki: (ki[p], 0)),
                    pl.BlockSpec((N_HEADS, tb, LANES), lambda p, qi, ki: (1, ki[p], 0))]
        args = (q, k, krp, v)
        scratch = scratch + [pltpu.VMEM((tb, tb), F32)]
    return pl.pallas_call(
        functools.partial(_flash_kernel, kind=kind, tb=tb),
        grid_spec=pltpu.PrefetchScalarGridSpec(
            num_scalar_prefetch=2,
            grid=(len(pairs),),
            in_specs=in_specs,
            out_specs=pl.BlockSpec((tb, N_HEADS * LANES), lambda p, qi, ki: (qi[p], 0)),
            scratch_shapes=scratch),
        out_shape=jax.ShapeDtypeStruct((seq, N_HEADS * LANES), F32),
        compiler_params=_params(("arbitrary",)),
        name="flash_" + kind,
    )(qi, ki, *args)


def _sample_kernel(ixq_ref, wb_ref, ikT_ref, aq_ref, kc_ref, vc_ref, kn_ref, vn_ref, bias_ref, qc_ref,
                   kvb_ref, krp_ref, oa_ref, ob_ref, *, n_new, s_valid, past, topk):
    sp = ikT_ref.shape[2]
    qpos = past + lax.broadcasted_iota(I32, (n_new, sp), 0)
    kpos = lax.broadcasted_iota(I32, (n_new, sp), 1)
    adm = (kpos < s_valid) & ((kpos >> CHUNK_SHIFT) <= (qpos >> CHUNK_SHIFT))

    qb = ixq_ref[...].astype(BF16)
    w = wb_ref[...]
    kT = ikT_ref[0]
    off = MISC_IXW - MISC_KRSW
    score = jnp.zeros((n_new, sp), F32)
    for h in range(IDX_HEADS):
        d = jnp.dot(qb[:, h * IDX_DIM:(h + 1) * IDX_DIM], kT, preferred_element_type=F32)
        score = score + jnp.maximum(d, 0.0) * (w[:, off + h:off + h + 1] * (IDX_HEADS ** -0.5))
    key = jnp.where(adm, _sort_key(score), KEY_MASKED)

    kf = float(topk)

    def count_ge(cand):
        return jnp.sum(jnp.where(key >= cand, 1.0, 0.0), axis=1, keepdims=True)

    ans = jnp.where(count_ge(jnp.zeros((n_new, 1), I32)) >= kf, 0, INT_MIN).astype(I32)

    def bit_step(b, ans):
        cand = ans + jnp.left_shift(jnp.int32(1), 30 - b)
        return jnp.where(count_ge(cand) >= kf, cand, ans)

    ans = lax.fori_loop(0, 31, bit_step, ans)
    need = kf - count_ge(ans + 1)
    eq = jnp.where(key == ans, 1.0, 0.0)
    jj = lax.broadcasted_iota(I32, (sp, sp), 0)
    ii = lax.broadcasted_iota(I32, (sp, sp), 1)
    before = jnp.where(jj < ii, 1.0, 0.0).astype(BF16)
    rank = jnp.dot(eq.astype(BF16), before, preferred_element_type=F32)
    sel = ((key > ans) | ((eq > 0.5) & (rank < need))) & (key > KEY_HALF)

    def attend(s, vmat):
        m = jnp.max(s, axis=1, keepdims=True)
        pr = jnp.exp2(s - m)
        l = jnp.sum(pr, axis=1, keepdims=True)
        return jnp.dot(pr.astype(BF16), vmat, preferred_element_type=F32) / l

    row_pad = jnp.zeros((sp - past - n_new, LANES), BF16)
    for h in range(N_HEADS):
        hs = slice(h * LANES, (h + 1) * LANES)
        kn = jnp.concatenate([kn_ref[h], row_pad], axis=0)
        vn = jnp.concatenate([vn_ref[h], row_pad], axis=0)
        q = aq_ref[h]
        s = jnp.concatenate([lax.dot_general(q, kc_ref[0, h], _NT, preferred_element_type=F32),
                             lax.dot_general(q, kn, _NT, preferred_element_type=F32)], axis=1)
        s = jnp.where(sel, s + bias_ref[h], NEG_INF)
        m = jnp.max(s, axis=1, keepdims=True)
        pr = jnp.exp2(s - m)
        l = jnp.sum(pr, axis=1, keepdims=True)
        pr = pr.astype(BF16)
        pv = (jnp.dot(pr[:, :past], vc_ref[0, h], preferred_element_type=F32)
              + jnp.dot(pr[:, past:], vn, preferred_element_type=F32))
        oa_ref[:, hs] = pv / l

    krp = krp_ref[0]
    hd = N_HEADS * LANES
    for h in range(N_HEADS):
        hs = slice(h * LANES, (h + 1) * LANES)
        k = jnp.concatenate([kvb_ref[0, :, hs], krp], axis=1)
        s = lax.dot_general(qc_ref[h], k, _NT, preferred_element_type=F32)
        s = jnp.where(adm, s, NEG_INF)
        ob_ref[:, hs] = attend(s, kvb_ref[0, :, hd + h * LANES:hd + (h + 1) * LANES])


def _sample_attn(zmisc, ixk_t, aq_hm, kc, vc, kn_hm, vn_hm, bias, qcat, kvb, krp, n_new, s_valid, past,
                 topk):
    nb = kc.shape[0]
    hd = N_HEADS * LANES
    assert past % LANES == 0 and n_new <= LANES and ixk_t.shape[2] == past + LANES
    new_rows = pl.BlockSpec((N_HEADS, n_new, LANES), lambda b: (0, b, 0))
    per_stream = lambda a: pl.BlockSpec((1,) + a.shape[1:], lambda b: (b,) + (0,) * (a.ndim - 1))
    return pl.pallas_call(
        functools.partial(_sample_kernel, n_new=n_new, s_valid=s_valid, past=past, topk=topk),
        grid=(nb,),
        in_specs=[pl.BlockSpec((n_new, IDX_HEADS * IDX_DIM), lambda b: (b, 0)),
                  pl.BlockSpec((n_new, LANES), lambda b: (b, MISC_KRSW // LANES)),
                  per_stream(ixk_t),
                  new_rows, per_stream(kc), per_stream(vc), new_rows, new_rows,
                  pl.BlockSpec(bias.shape, lambda b: (0, 0, 0)),
                  pl.BlockSpec((N_HEADS, n_new, 256), lambda b: (0, b, 0)),
                  per_stream(kvb), per_stream(krp)],
        out_specs=[pl.BlockSpec((n_new, hd), lambda b: (b, 0)),
                   pl.BlockSpec((n_new, hd), lambda b: (b, 0))],
        out_shape=[jax.ShapeDtypeStruct((nb * n_new, hd), F32),
                   jax.ShapeDtypeStruct((nb * n_new, hd), F32)],
        compiler_params=_params(("parallel",)),
        name="sample_attn",
    )(zmisc, zmisc, ixk_t, aq_hm, kc, vc, kn_hm, vn_hm, bias, qcat, kvb, krp)


def _outproj_kernel(x_ref, ga_ref, gb_ref, oa_ref, ob_ref, w_ref, o_ref):
    mix = ga_ref[...] * oa_ref[...] + gb_ref[...] * ob_ref[...]
    o_ref[...] = x_ref[...] + jnp.dot(mix.astype(BF16), w_ref[...], preferred_element_type=F32)


def _outproj(x, ga, gb, oa, ob, w, tm):
    m, d = x.shape
    row = pl.BlockSpec((tm, d), lambda i: (i, 0))
    return pl.pallas_call(
        _outproj_kernel,
        grid=(m // tm,),
        in_specs=[row, row, row, row, row, pl.BlockSpec(w.shape, lambda i: (0, 0))],
        out_specs=row,
        out_shape=jax.ShapeDtypeStruct((m, d), F32),
        compiler_params=_params(("parallel",)),
        name="outproj",
    )(x, ga, gb, oa, ob, w)


def _ffn_kernel(x_ref, g_ref, wu_ref, wd_ref, gf_ref, o_ref, h_sc, acc_sc):
    f = pl.program_id(1)

    @pl.when(f == 0)
    def _start():
        h_sc[...] = _rms(x_ref[...], g_ref[...]).astype(BF16)
        acc_sc[...] = jnp.zeros(acc_sc.shape, F32)

    u = jnp.dot(h_sc[...], wu_ref[...], preferred_element_type=F32)
    u = jnp.square(jnp.maximum(u, 0.0)).astype(BF16)
    acc_sc[...] += jnp.dot(u, wd_ref[...], preferred_element_type=F32)

    @pl.when(f == pl.num_programs(1) - 1)
    def _end():
        o_ref[...] = _rms(x_ref[...] + acc_sc[...], gf_ref[...])


def _ffn(x, g, wu, wd, gf, tm, tf=512):
    m, d = x.shape
    dff = wu.shape[1]
    return pl.pallas_call(
        _ffn_kernel,
        grid=(m // tm, dff // tf),
        in_specs=[pl.BlockSpec((tm, d), lambda i, f: (i, 0)),
                  pl.BlockSpec((1, d), lambda i, f: (0, 0)),
                  pl.BlockSpec((d, tf), lambda i, f: (0, f)),
                  pl.BlockSpec((tf, d), lambda i, f: (f, 0)),
                  pl.BlockSpec((1, d), lambda i, f: (0, 0))],
        out_specs=pl.BlockSpec((tm, d), lambda i, f: (i, 0)),
        out_shape=jax.ShapeDtypeStruct((m, d), F32),
        scratch_shapes=[pltpu.VMEM((tm, d), BF16), pltpu.VMEM((tm, d), F32)],
        compiler_params=_params(("parallel", "arbitrary")),
        name="ffn",
    )(x, g.reshape(1, d), wu, wd, gf.reshape(1, d))


def _t5_bucket(rel):
    nb = REL_BUCKETS // 2
    ret = (rel > 0).astype(jnp.int32) * nb
    n = jnp.abs(rel)
    max_exact = nb // 2
    large = max_exact + (jnp.log(jnp.maximum(n, 1).astype(jnp.float32) / max_exact)
                         / math.log(REL_MAX_DIST / max_exact) * (nb - max_exact)).astype(jnp.int32)
    large = jnp.minimum(large, nb - 1)
    return ret + jnp.where(n < max_exact, n, large)


def _rel_bias(table, rel):
    onehot = jax.nn.one_hot(_t5_bucket(rel), REL_BUCKETS, dtype=F32)
    return jnp.einsum("...b,bh->...h", onehot, table.astype(F32), precision=lax.Precision.HIGHEST)


def _rope_tables(pos):
    half = ROPE_DIM // 2
    inv_freq = jnp.power(ROPE_THETA, -jnp.arange(half, dtype=jnp.float32) / half)
    ang = pos.astype(jnp.float32)[:, None] * inv_freq[None, :]
    cos, sin = jnp.cos(ang), jnp.sin(ang)
    pad = jnp.zeros((pos.shape[0], LANES - ROPE_DIM), F32)
    return (jnp.concatenate([cos, cos, pad], axis=1), jnp.concatenate([-sin, sin, pad], axis=1))


def _swap_halves(w):
    half = ROPE_DIM // 2
    return jnp.concatenate([w[..., half:], w[..., :half]], axis=-1)


def _prep_weights(w_in, w_uq, w_uk, w_uv, w_out, w_ff_up, w_ff_down):
    d = w_in.shape[0]
    hd = N_HEADS * HEAD_DIM
    sizes = (hd, hd, hd, IDX_HEADS * IDX_DIM, IDX_DIM, IDX_HEADS, Q_LORA, KV_LORA, ROPE_DIM, d, d)
    cols, off = [], 0
    for n in sizes:
        cols.append(w_in[:, off:off + n])
        off += n
    wq, wk, wv, wixq, wixk, wixw, wcq, wckv, wkr, wga, wgb = cols
    pad = jnp.zeros((d, MISC_COLS - MISC_IXW - IDX_HEADS), w_in.dtype)
    wmisc = jnp.concatenate([wixq, wcq, wckv, wkr, wixk, _swap_halves(wkr), wixw, pad], axis=1)
    nope, rope = w_uq[..., :QK_NOPE_DIM], w_uq[..., QK_NOPE_DIM:]
    zpad = jnp.zeros(rope.shape, w_uq.dtype)
    wq_cat = jnp.concatenate([nope, rope, zpad], axis=-1).reshape(Q_LORA, N_HEADS * 256)
    wq_sw = jnp.concatenate([_swap_halves(rope), zpad], axis=-1).reshape(Q_LORA, N_HEADS * LANES)
    wkv = jnp.concatenate([w_uk.reshape(KV_LORA, hd), w_uv.reshape(KV_LORA, hd)], axis=1)
    c = lambda a: a.astype(BF16)
    return dict(q=c(wq), k=c(wk), v=c(wv), misc=c(wmisc), ga=c(wga), gb=c(wgb), wq_cat=c(wq_cat),
                wq_sw=c(wq_sw), wkv=c(wkv), out=c(w_out), up=c(w_ff_up), down=c(w_ff_down))


def _front(x, pos, w, g_mix, g_q, g_kv, tm):
    h = _norm_cast(x, g_mix, tm)
    aq_hm = _mm(h, w["q"], "heads", tm, head_scale=A_SCALE * LOG2E)
    ak, ak_hm = _mm(h, w["k"], "f32+heads", tm)
    av, av_hm = _mm(h, w["v"], "f32+heads", tm)
    zmisc = _mm(h, w["misc"], "f32", tm)
    sga = _mm(h, w["ga"], "sigmoid", tm)
    sgb = _mm(h, w["gb"], "sigmoid", tm)
    cos_t, sin_t = _rope_tables(pos)
    qcat, ckv, ckv_b, kr, krp = _mla_prep(zmisc, cos_t, sin_t, g_q, g_kv, w["wq_cat"], w["wq_sw"], tm)
    return dict(aq_hm=aq_hm, ak=ak, ak_hm=ak_hm, av=av, av_hm=av_hm, zmisc=zmisc, sga=sga, sgb=sgb,
                qcat=qcat, ckv=ckv, ckv_b=ckv_b, kr=kr, krp=krp)


def _back(x, f, oa, ob, w, g_ffn, g_final, tm_out, tm_ffn):
    x1 = _outproj(x, f["sga"], f["sgb"], oa, ob, w["out"], tm_out)
    return _ffn(x1, g_ffn, w["up"], w["down"], g_final, tm_ffn)


def kernel(x_prompt, x_sample, cache_a_k, cache_a_v, cache_a_idx_k, cache_b_ckv, cache_b_krope,
           rel_bias_table, norm_mix_g, w_in, q_lora_g, w_uq, kv_lora_g, w_uk, w_uv, w_out,
           norm_ffn_g, w_ff_up, w_ff_down, final_norm_g):
    assert w_in.shape[0] == 1, "single-layer trunk"
    _, seq, d = x_prompt.shape
    nb, n_new, _ = x_sample.shape
    past = cache_a_k.shape[2]
    s_valid = past + n_new
    sp = -(-s_valid // LANES) * LANES
    topk_p = min(TOPK_MAX, seq // 4)
    topk_s = min(TOPK_MAX, s_valid // 4)
    hd = N_HEADS * HEAD_DIM

    w = _prep_weights(w_in[0], w_uq[0], w_uk[0], w_uv[0], w_out[0], w_ff_up[0], w_ff_down[0])

    xp = x_prompt.reshape(seq, d)
    fp = _front(xp, jnp.arange(seq, dtype=jnp.int32), w, norm_mix_g[0], q_lora_g[0], kv_lora_g[0], 512)
    ixk_p = fp["zmisc"][:, MISC_IXK:MISC_IXK + IDX_DIM]
    mask = _indexer(fp["zmisc"], ixk_p.astype(BF16).T, topk_p)
    ii = jnp.arange(LANES, dtype=jnp.int32)
    rel0 = ii[None, :] - ii[:, None]
    far = _rel_bias(rel_bias_table, jnp.full((1, 1), -REL_MAX_DIST, jnp.int32))
    btiles = jnp.stack([_rel_bias(rel_bias_table, rel0) - far,
                        _rel_bias(rel_bias_table, rel0 - LANES) - far,
                        jnp.zeros((LANES, LANES, N_HEADS), F32)])
    btiles = jnp.transpose(btiles, (0, 3, 1, 2)).astype(F32) * LOG2E
    oa_p = _flash("dsa", fp["aq_hm"], fp["ak_hm"], fp["av_hm"], (mask, btiles), seq)
    kvb_p = _mm(fp["ckv_b"], w["wkv"], "heads", 512)
    ob_p = _flash("mla", fp["qcat"], kvb_p, kvb_p, (fp["krp"],), seq)
    y_p = _back(xp, fp, oa_p, ob_p, w, norm_ffn_g[0], final_norm_g, 256, 512)

    m_s = nb * n_new
    xs = x_sample.reshape(m_s, d)
    pos_s = jnp.tile(past + jnp.arange(n_new, dtype=jnp.int32), nb)
    fs = _front(xs, pos_s, w, norm_mix_g[0], q_lora_g[0], kv_lora_g[0], m_s)
    ixk_s = fs["zmisc"][:, MISC_IXK:MISC_IXK + IDX_DIM]

    def with_cache(cache, new, width):
        a = jnp.concatenate([cache.reshape(nb, past, width).astype(BF16),
                             new.reshape(nb, n_new, width).astype(BF16)], axis=1)
        return jnp.pad(a, ((0, 0), (0, sp - s_valid), (0, 0)))

    kc = jnp.transpose(cache_a_k[0].astype(BF16), (0, 2, 1, 3))
    vc = jnp.transpose(cache_a_v[0].astype(BF16), (0, 2, 1, 3))
    ixk_all = jnp.swapaxes(with_cache(cache_a_idx_k[0], ixk_s, IDX_DIM), 1, 2)
    ckv_all = with_cache(cache_b_ckv[0], fs["ckv"], KV_LORA)
    krope_pad = jnp.pad(cache_b_krope[0], ((0, 0), (0, 0), (0, LANES - ROPE_DIM)))
    krp_all = with_cache(krope_pad, fs["krp"], LANES)
    kvb_s = _mm(ckv_all.reshape(nb * sp, KV_LORA), w["wkv"], "bf16", sp).reshape(nb, sp, 2 * hd)
    qpos = past + jnp.arange(n_new, dtype=jnp.int32)
    kpos = jnp.arange(sp, dtype=jnp.int32)
    bias_s = jnp.transpose(_rel_bias(rel_bias_table, kpos[None, :] - qpos[:, None]), (2, 0, 1))
    oa_s, ob_s = _sample_attn(fs["zmisc"], ixk_all, fs["aq_hm"], kc, vc, fs["ak_hm"], fs["av_hm"],
                              bias_s.astype(F32) * LOG2E,
                              fs["qcat"], kvb_s, krp_all, n_new, s_valid, past, topk_s)
    y_s = _back(xs, fs, oa_s, ob_s, w, norm_ffn_g[0], final_norm_g, m_s, m_s)

    st = lambda a, b, *tail: a.reshape((1, b, -1) + tail)
    return (y_p.reshape(1, seq, d), y_s.reshape(nb, n_new, d),
            st(fp["ak"], 1, N_HEADS, HEAD_DIM), st(fp["av"], 1, N_HEADS, HEAD_DIM),
            st(ixk_p, 1, IDX_DIM), st(fp["ckv"], 1, KV_LORA), st(fp["kr"], 1, ROPE_DIM),
            st(fs["ak"], nb, N_HEADS, HEAD_DIM), st(fs["av"], nb, N_HEADS, HEAD_DIM),
            st(ixk_s, nb, IDX_DIM), st(fs["ckv"], nb, KV_LORA), st(fs["kr"], nb, ROPE_DIM))
```

```python
import functools
import math

import numpy as np
import jax
import jax.numpy as jnp
from jax import lax
from jax.experimental import pallas as pl
from jax.experimental.pallas import tpu as pltpu

F32 = jnp.float32
BF16 = jnp.bfloat16
I32 = jnp.int32

CHUNK = 64
CHUNK_SHIFT = 6
N_HEADS = 16
HEAD_DIM = 128
IDX_HEADS = 16
IDX_DIM = 64
TOPK_MAX = 256
REL_BUCKETS = 32
REL_MAX_DIST = 128
QK_NOPE_DIM = 128
ROPE_DIM = 64
Q_LORA = 512
KV_LORA = 256
ROPE_THETA = 10000.0
MLA_SCALE = (QK_NOPE_DIM + ROPE_DIM) ** -0.5
A_SCALE = HEAD_DIM ** -0.5
EPS = 1e-6
NEG_INF = -1e30
LOG2E = math.log2(math.e)

LANES = 128
Q_CAT = 2 * LANES
V7X_VMEM_LIMIT = 56 * 1024 * 1024

MISC_IXQ = 0
MISC_CQ = 1024
MISC_CKV = 1536
MISC_KR = 1792
MISC_IXK = 1856
MISC_KRSW = 1920
MISC_IXW = 1984
MISC_COLS = 2048


def _sortable_key_const(v):
    bits = int(np.float32(v).view(np.int32))
    return bits ^ ((bits >> 31) & 0x7FFFFFFF)


KEY_MASKED = _sortable_key_const(NEG_INF)
KEY_HALF = _sortable_key_const(np.float32(0.5) * np.float32(NEG_INF))
INT_MIN = -(2 ** 31)
HI16_MASK = -65536


def _params(sem):
    return pltpu.CompilerParams(dimension_semantics=sem, vmem_limit_bytes=V7X_VMEM_LIMIT)


def _rms(x, g):
    ms = jnp.mean(x * x, axis=-1, keepdims=True)
    return (x * lax.rsqrt(ms + EPS)) * g


def _sort_key(x):
    bits = lax.bitcast_convert_type(x, I32)
    return bits ^ ((bits >> 31) & 0x7FFFFFFF)


def _lane_tile(x, n):
    return x if n == 1 else jnp.concatenate([x] * n, axis=1)


def _norm_cast_kernel(x_ref, g_ref, o_ref):
    o_ref[...] = _rms(x_ref[...], g_ref[...]).astype(o_ref.dtype)


def _norm_cast(x, g, tm):
    m, d = x.shape
    return pl.pallas_call(
        _norm_cast_kernel,
        grid=(m // tm,),
        in_specs=[pl.BlockSpec((tm, d), lambda i: (i, 0)),
                  pl.BlockSpec((1, d), lambda i: (0, 0))],
        out_specs=pl.BlockSpec((tm, d), lambda i: (i, 0)),
        out_shape=jax.ShapeDtypeStruct((m, d), BF16),
        compiler_params=_params(("parallel",)),
        name="norm_cast",
    )(x, g.reshape(1, d))


def _mm_kernel(x_ref, w_ref, *o_refs, mode, head_scale):
    acc = jnp.dot(x_ref[...], w_ref[...], preferred_element_type=F32)
    if mode == "f32":
        o_refs[0][...] = acc
    elif mode == "sigmoid":
        o_refs[0][...] = jax.nn.sigmoid(acc)
    elif mode == "bf16":
        o_refs[0][...] = acc.astype(BF16)
    else:
        if mode == "f32+heads":
            o_refs[0][...] = acc
        hm = o_refs[-1]
        for h in range(hm.shape[0]):
            blk = acc[:, h * LANES:(h + 1) * LANES]
            hm[h] = (blk if head_scale == 1.0 else blk * head_scale).astype(BF16)


def _mm(x, w, mode, tm, head_scale=1.0):
    m, k = x.shape
    n = w.shape[1]
    flat = pl.BlockSpec((tm, n), lambda i: (i, 0))
    heads = pl.BlockSpec((n // LANES, tm, LANES), lambda i: (0, i, 0))
    hm_shape = jax.ShapeDtypeStruct((n // LANES, m, LANES), BF16)
    if mode in ("f32", "sigmoid"):
        out_specs, out_shape = flat, jax.ShapeDtypeStruct((m, n), F32)
    elif mode == "bf16":
        out_specs, out_shape = flat, jax.ShapeDtypeStruct((m, n), BF16)
    elif mode == "heads":
        out_specs, out_shape = heads, hm_shape
    else:
        out_specs = [flat, heads]
        out_shape = [jax.ShapeDtypeStruct((m, n), F32), hm_shape]
    return pl.pallas_call(
        functools.partial(_mm_kernel, mode=mode, head_scale=head_scale),
        grid=(m // tm,),
        in_specs=[pl.BlockSpec((tm, k), lambda i: (i, 0)),
                  pl.BlockSpec((k, n), lambda i: (0, 0))],
        out_specs=out_specs,
        out_shape=out_shape,
        compiler_params=_params(("parallel",)),
        name="mm_" + mode.replace("+", "_"),
    )(x, w)


def _mla_prep_kernel(cq_ref, ckv_ref, a_ref, b_ref, cos_ref, sin_ref, gq_ref, gkv_ref,
                     wq_ref, wsw_ref, qcat_ref, ckv_o, ckvb_o, kr_o, krp_o):
    cq = _rms(cq_ref[...], gq_ref[...]).astype(BF16)
    ckv = _rms(ckv_ref[...], gkv_ref[...])
    ckv_o[...] = ckv
    ckvb_o[...] = ckv.astype(BF16)
    cos = cos_ref[...]
    sin = sin_ref[...]
    r = a_ref[...] * cos + b_ref[...] * sin
    kr_o[...] = r[:, :ROPE_DIM]
    krp_o[...] = r.astype(BF16)
    for h in range(N_HEADS):
        qh = jnp.dot(cq, wq_ref[:, h * Q_CAT:(h + 1) * Q_CAT], preferred_element_type=F32)
        qs = jnp.dot(cq, wsw_ref[:, h * LANES:(h + 1) * LANES], preferred_element_type=F32)
        rp = qh[:, LANES:] * cos + qs * sin
        qcat = jnp.concatenate([qh[:, :LANES], rp], axis=1) * (MLA_SCALE * LOG2E)
        qcat_ref[h] = qcat.astype(BF16)


def _mla_prep(zmisc, cos_t, sin_t, gq, gkv, wq_cat, wq_sw, tm):
    m = zmisc.shape[0]
    row = lambda w, j: pl.BlockSpec((tm, w), lambda i, j=j: (i, j))
    const = lambda a: pl.BlockSpec(a.shape, lambda i: (0,) * a.ndim)
    gq2, gkv2 = gq.reshape(1, Q_LORA), gkv.reshape(1, KV_LORA)
    return pl.pallas_call(
        _mla_prep_kernel,
        grid=(m // tm,),
        in_specs=[row(Q_LORA, MISC_CQ // Q_LORA), row(KV_LORA, MISC_CKV // KV_LORA),
                  row(LANES, MISC_KR // LANES), row(LANES, MISC_KRSW // LANES),
                  row(LANES, 0), row(LANES, 0), const(gq2), const(gkv2),
                  const(wq_cat), const(wq_sw)],
        out_specs=[pl.BlockSpec((N_HEADS, tm, Q_CAT), lambda i: (0, i, 0)),
                   row(KV_LORA, 0), row(KV_LORA, 0), row(ROPE_DIM, 0), row(LANES, 0)],
        out_shape=[jax.ShapeDtypeStruct((N_HEADS, m, Q_CAT), BF16),
                   jax.ShapeDtypeStruct((m, KV_LORA), F32),
                   jax.ShapeDtypeStruct((m, KV_LORA), BF16),
                   jax.ShapeDtypeStruct((m, ROPE_DIM), F32),
                   jax.ShapeDtypeStruct((m, LANES), BF16)],
        compiler_params=_params(("parallel",)),
        name="mla_prep",
    )(zmisc, zmisc, zmisc, zmisc, cos_t, sin_t, gq2, gkv2, wq_cat, wq_sw)


SCORE_TILE = 256


def _indexer_kernel(q_ref, wb_ref, kT_ref, mask_ref, key_sc, co_sc, w_sc, q_sc, *, tq, ts, cw, topk):
    i = pl.program_id(0)
    s_total = kT_ref.shape[2]
    t0 = i * tq
    front = t0 + tq
    nblk = (front + ts - 1) // ts
    nch = nblk * (ts // cw)

    w = wb_ref[...]
    off = MISC_IXW - MISC_KRSW
    for h in range(IDX_HEADS):
        w_sc[h] = jnp.broadcast_to(w[:, off + h:off + h + 1] * (IDX_HEADS ** -0.5), (tq, LANES))
    qb = q_ref[...].astype(BF16)
    npair = IDX_HEADS // 2
    for j in range(npair):
        q_sc[j * tq:(j + 1) * tq, :] = qb[:, j * LANES:(j + 1) * LANES]

    def score_block(kb, carry):
        c0 = pl.multiple_of(kb * ts, ts)
        for c in range(ts // SCORE_TILE):
            cc = pl.multiple_of(c0 + c * SCORE_TILE, SCORE_TILE)
            q_all = q_sc[...]
            d_par = [jnp.dot(q_all, kT_ref[par, :, pl.ds(cc, SCORE_TILE)], preferred_element_type=F32)
                     for par in range(2)]
            acc = jnp.zeros((tq, SCORE_TILE), F32)
            for h in range(IDX_HEADS):
                d = d_par[h % 2][(h // 2) * tq:(h // 2 + 1) * tq]
                acc = acc + jnp.maximum(d, 0.0) * _lane_tile(w_sc[h], SCORE_TILE // LANES)
            rows = t0 + lax.broadcasted_iota(I32, (tq, SCORE_TILE), 0)
            cols = cc + lax.broadcasted_iota(I32, (tq, SCORE_TILE), 1)
            adm = (cols >> CHUNK_SHIFT) <= (rows >> CHUNK_SHIFT)
            sc = jnp.where(adm, acc, NEG_INF)
            bits = lax.bitcast_convert_type(sc, I32)
            key_sc[:, pl.ds(cc, SCORE_TILE)] = bits ^ ((bits >> 31) & 0x7FFFFFFF)
            hi = lax.bitcast_convert_type(bits & HI16_MASK, F32)
            co_sc[:, pl.ds(cc, SCORE_TILE)] = hi.astype(BF16)
        return carry

    lax.fori_loop(0, nblk, score_block, 0)

    kf = float(topk)

    def count_ge(cand):
        def body(c, cnt):
            c0 = pl.multiple_of(c * cw, cw)
            k = key_sc[:, pl.ds(c0, cw)]
            for j in range(cw // LANES):
                cnt = cnt + jnp.where(k[:, j * LANES:(j + 1) * LANES] >= cand, 1.0, 0.0)
            return cnt
        cnt = lax.fori_loop(0, nch, body, jnp.zeros((tq, LANES), F32))
        return jnp.broadcast_to(jnp.sum(cnt, axis=1, keepdims=True), (tq, LANES))

    def pattern16(c16):
        return jnp.where(c16 >= 0, c16, c16 ^ 0x7FFF) & 0xFFFF

    def count_ge_coarse(c16):
        cand = lax.bitcast_convert_type(pattern16(c16) << 16, F32).astype(BF16)

        def body(c, cnt):
            c0 = pl.multiple_of(c * cw, cw)
            x = co_sc[:, pl.ds(c0, cw)]
            for j in range(cw // LANES):
                hit = x[:, j * LANES:(j + 1) * LANES] >= cand
                cnt = cnt + jnp.where(hit, jnp.ones((), BF16), jnp.zeros((), BF16))
            return cnt
        cnt = lax.fori_loop(0, nch, body, jnp.zeros((tq, LANES), BF16))
        total = jnp.sum(cnt.astype(F32), axis=1, keepdims=True)
        return jnp.broadcast_to(total, (tq, LANES))

    c0cnt = count_ge_coarse(jnp.zeros((tq, LANES), I32))
    pos = c0cnt >= kf
    t16 = jnp.where(pos, 0, -32768).astype(I32)
    cnt_lo = jnp.where(pos, c0cnt, float(s_total))

    def coarse_step(b, st):
        t16, cnt_lo = st
        cand = t16 + jnp.left_shift(jnp.int32(1), 14 - b)
        cnt = count_ge_coarse(cand)
        ok = cnt >= kf
        return jnp.where(ok, cand, t16), jnp.where(ok, cnt, cnt_lo)

    t16, cnt_lo = lax.fori_loop(0, 15, coarse_step, (t16, cnt_lo))

    pat = pattern16(t16)
    b0 = pat << 16
    b1 = b0 | 0xFFFF
    k0 = b0 ^ ((b0 >> 31) & 0x7FFFFFFF)
    k1 = b1 ^ ((b1 >> 31) & 0x7FFFFFFF)
    is_zero = (pat & 0x7FFF) == 0
    lo = jnp.where(is_zero, -65536, jnp.minimum(k0, k1))
    hi = jnp.where(is_zero, 65535, jnp.maximum(k0, k1))
    rows1 = t0 + lax.broadcasted_iota(I32, (tq, LANES), 0)
    few = (((rows1 >> CHUNK_SHIFT) + 1) << CHUNK_SHIFT) < topk

    def active_of(lo, hi, cnt_lo):
        done = few | (cnt_lo == kf) | (lo >= hi)
        return jnp.max(jnp.where(done, 0.0, 1.0))

    def fine_cond(st):
        return st[3] > 0.5

    def fine_step(st):
        lo, hi, cnt_lo, _ = st
        done = few | (cnt_lo == kf) | (lo >= hi)
        mid = lo + ((hi - lo + 1) >> 1)
        cnt = count_ge(mid)
        up = (cnt >= kf) & jnp.logical_not(done)
        down = (cnt < kf) & jnp.logical_not(done)
        lo = jnp.where(up, mid, lo)
        cnt_lo = jnp.where(up, cnt, cnt_lo)
        hi = jnp.where(down, mid - 1, hi)
        return lo, hi, cnt_lo, active_of(lo, hi, cnt_lo)

    lo, hi, cnt_lo, _ = lax.while_loop(fine_cond, fine_step, (lo, hi, cnt_lo, active_of(lo, hi, cnt_lo)))
    ans = jnp.where(few, INT_MIN, lo)
    thr = _lane_tile(jnp.maximum(ans, KEY_HALF + 1), cw // LANES)
    tied = jnp.logical_not(few) & (cnt_lo > kf) & (lo > KEY_HALF)
    any_tied = jnp.max(jnp.where(tied, 1.0, 0.0))

    def write_mask(c, carry):
        c0 = pl.multiple_of(c * cw, cw)
        sel = key_sc[:, pl.ds(c0, cw)] >= thr
        mask_ref[:, pl.ds(c0, cw)] = jnp.where(sel, 1, 0).astype(jnp.int8)
        return carry

    @pl.when(any_tied < 0.5)
    def _plain():
        lax.fori_loop(0, nch, write_mask, 0)

    @pl.when(any_tied >= 0.5)
    def _with_ties():
        need = kf - count_ge(lo + 1)
        lo_t = _lane_tile(lo, cw // LANES)
        tied_t = _lane_tile(jnp.where(tied, 1.0, 0.0), cw // LANES) > 0.5
        need_t = _lane_tile(need, cw // LANES)
        jj = lax.broadcasted_iota(I32, (cw, cw), 0)
        ii = lax.broadcasted_iota(I32, (cw, cw), 1)
        upper = jnp.where(jj <= ii, 1.0, 0.0).astype(BF16)

        def write_tied(c, seen):
            c0 = pl.multiple_of(c * cw, cw)
            k = key_sc[:, pl.ds(c0, cw)]
            eq = jnp.where(k == lo_t, 1.0, 0.0)
            incl = jnp.dot(eq.astype(BF16), upper, preferred_element_type=F32)
            rank = _lane_tile(seen, cw // LANES) + incl - eq
            take = (k > lo_t) | ((eq > 0.5) & (rank < need_t))
            sel = (tied_t & take) | (jnp.logical_not(tied_t) & (k >= thr))
            mask_ref[:, pl.ds(c0, cw)] = jnp.where(sel, 1, 0).astype(jnp.int8)
            return seen + jnp.broadcast_to(incl[:, cw - 1:cw], (tq, LANES))

        lax.fori_loop(0, nch, write_tied, jnp.zeros((tq, LANES), F32))

    def zero_mask(c, carry):
        c0 = pl.multiple_of(c * cw, cw)
        mask_ref[:, pl.ds(c0, cw)] = jnp.zeros((tq, cw), jnp.int8)
        return carry

    lax.fori_loop(nch, s_total // cw, zero_mask, 0)


def _indexer(zmisc, ixk_t, topk, tq=128, ts=1024, cw=512):
    s = ixk_t.shape[1]
    assert ts % cw == 0 and s % ts == 0 and ts % tq == 0
    zero = jnp.zeros_like(ixk_t)
    ixk_t = jnp.stack([jnp.concatenate([ixk_t, zero], axis=0), jnp.concatenate([zero, ixk_t], axis=0)])
    return pl.pallas_call(
        functools.partial(_indexer_kernel, tq=tq, ts=ts, cw=cw, topk=topk),
        grid=(s // tq,),
        in_specs=[pl.BlockSpec((tq, IDX_HEADS * IDX_DIM), lambda i: (i, 0)),
                  pl.BlockSpec((tq, LANES), lambda i: (i, MISC_KRSW // LANES)),
                  pl.BlockSpec((2, 2 * IDX_DIM, s), lambda i: (0, 0, 0))],
        out_specs=pl.BlockSpec((tq, s), lambda i: (i, 0)),
        out_shape=jax.ShapeDtypeStruct((s, s), jnp.int8),
        scratch_shapes=[pltpu.VMEM((tq, s), I32), pltpu.VMEM((tq, s), BF16),
                        pltpu.VMEM((IDX_HEADS, tq, LANES), F32),
                        pltpu.VMEM((IDX_HEADS // 2 * tq, 2 * IDX_DIM), BF16)],
        compiler_params=_params(("parallel",)),
        name="indexer",
    )(zmisc, zmisc, ixk_t)


_NT = (((1,), (1,)), ((), ()))
FLASH_AHEAD = 2
FLASH_GROUP = 16


def _flash_kernel(qi_ref, ki_ref, *refs, kind, tb):
    if kind == "dsa":
        q_ref, k_ref, v_ref, mask_ref, b_ref, o_ref, m_sc, l_sc, acc_sc, mb_sc = refs
    else:
        q_ref, k_ref, kr_ref, v_ref, o_ref, m_sc, l_sc, acc_sc, mb_sc = refs
    p = pl.program_id(0)
    qi = qi_ref[p]
    ki = ki_ref[p]
    nsub = tb // LANES

    @pl.when(ki == 0)
    def _init():
        m_sc[...] = jnp.full(m_sc.shape, NEG_INF, F32)
        l_sc[...] = jnp.zeros(l_sc.shape, F32)
        acc_sc[...] = jnp.zeros(acc_sc.shape, F32)

    ones = jnp.ones((tb, LANES), BF16)

    def update(h, s):
        m_prev = m_sc[h]
        m_new = jnp.maximum(m_prev, jnp.max(s, axis=1, keepdims=True))
        alpha = jnp.exp2(m_prev - m_new)
        pr = jnp.exp2(s - _lane_tile(m_new, nsub)).astype(BF16)
        pv = jnp.dot(pr, jnp.concatenate([v_ref[h], ones], axis=1), preferred_element_type=F32)
        acc_sc[h] = alpha * acc_sc[h] + pv[:, :LANES]
        l_sc[h] = alpha * l_sc[h] + pv[:, LANES:]
        m_sc[h] = m_new

    def run_heads(logits):
        def group(g, carry):
            base = g * FLASH_GROUP
            ahead = [logits(base + j) for j in range(min(FLASH_AHEAD, FLASH_GROUP))]
            for j in range(FLASH_GROUP):
                s = ahead.pop(0)
                if j + FLASH_AHEAD < FLASH_GROUP:
                    ahead.append(logits(base + j + FLASH_AHEAD))
                update(base + j, s)
            return carry

        lax.fori_loop(0, N_HEADS // FLASH_GROUP, group, 0)

    if kind == "dsa":
        mb_sc[...] = (mask_ref[...].astype(F32) - 1.0) * (-NEG_INF)

        def logits_far(h):
            return lax.dot_general(q_ref[h], k_ref[h], _NT, preferred_element_type=F32) + mb_sc[...]

        on_diag = ki == qi
        i_main = jnp.where(on_diag, 0, 2)
        i_sub = jnp.where(on_diag, 1, 2)
        i_corner = jnp.where(on_diag, 2, 1)

        def logits_near(h):
            s = logits_far(h)
            rows = []
            for a in range(nsub):
                blocks = []
                for b in range(nsub):
                    blk = s[a * LANES:(a + 1) * LANES, b * LANES:(b + 1) * LANES]
                    if b == a:
                        blk = blk + b_ref[i_main, h]
                    elif b == a - 1:
                        blk = blk + b_ref[i_sub, h]
                    elif a == 0 and b == nsub - 1:
                        blk = blk + b_ref[i_corner, h]
                    blocks.append(blk)
                rows.append(jnp.concatenate(blocks, axis=1))
            return jnp.concatenate(rows, axis=0)

        @pl.when(ki < qi - 1)
        def _far():
            run_heads(logits_far)

        @pl.when(ki >= qi - 1)
        def _near():
            run_heads(logits_near)
    else:
        @pl.when(p == 0)
        def _masks():
            rows = lax.broadcasted_iota(I32, (tb, tb), 0)
            cols = lax.broadcasted_iota(I32, (tb, tb), 1)
            vis = (cols >> CHUNK_SHIFT) <= (rows >> CHUNK_SHIFT)
            mb_sc[...] = jnp.where(vis, 0.0, NEG_INF)

        def qk(h):
            k = jnp.concatenate([k_ref[h], kr_ref[...]], axis=1)
            return lax.dot_general(q_ref[h], k, _NT, preferred_element_type=F32)

        @pl.when(ki < qi)
        def _off():
            run_heads(qk)

        @pl.when(ki == qi)
        def _diag():
            run_heads(lambda h: qk(h) + mb_sc[...])

    @pl.when(ki == qi)
    def _finish():
        for h in range(N_HEADS):
            o_ref[:, h * LANES:(h + 1) * LANES] = acc_sc[h] / l_sc[h]


def _flash(kind, q, k, v, extra, seq, tb=512):
    nq = seq // tb
    pairs = [(a, b) for a in range(nq) for b in range(a + 1)]
    qi = jnp.asarray([a for a, _ in pairs], I32)
    ki = jnp.asarray([b for _, b in pairs], I32)
    dq = q.shape[2]
    qspec = pl.BlockSpec((N_HEADS, tb, dq), lambda p, qi, ki: (0, qi[p], 0))
    scratch = [pltpu.VMEM((N_HEADS, tb, LANES), F32), pltpu.VMEM((N_HEADS, tb, LANES), F32),
               pltpu.VMEM((N_HEADS, tb, LANES), F32)]
    if kind == "dsa":
        mask, btiles = extra
        in_specs = [qspec,
                    pl.BlockSpec((N_HEADS, tb, LANES), lambda p, qi, ki: (0, ki[p], 0)),
                    pl.BlockSpec((N_HEADS, tb, LANES), lambda p, qi, ki: (0, ki[p], 0)),
                    pl.BlockSpec((tb, tb), lambda p, qi, ki: (qi[p], ki[p])),
                    pl.BlockSpec(btiles.shape, lambda p, qi, ki: (0, 0, 0, 0))]
        args = (q, k, v, mask, btiles)
        scratch = scratch + [pltpu.VMEM((tb, tb), F32)]
    else:
        (krp,) = extra
        in_specs = [qspec,
                    pl.BlockSpec((N_HEADS, tb, LANES), lambda p, qi, ki: (0, ki[p], 0)),
                    pl.BlockSpec((tb, LANES), lambda p, qi, ki: (ki[p], 0)),
                    pl.BlockSpec((N_HEADS, tb, LANES), lambda p, qi, ki: (1, ki[p], 0))]
        args = (q, k, krp, v)
        scratch = scratch + [pltpu.VMEM((tb, tb), F32)]
    return pl.pallas_call(
        functools.partial(_flash_kernel, kind=kind, tb=tb),
        grid_spec=pltpu.PrefetchScalarGridSpec(
            num_scalar_prefetch=2,
            grid=(len(pairs),),
            in_specs=in_specs,
            out_specs=pl.BlockSpec((tb, N_HEADS * LANES), lambda p, qi, ki: (qi[p], 0)),
            scratch_shapes=scratch),
        out_shape=jax.ShapeDtypeStruct((seq, N_HEADS * LANES), F32),
        compiler_params=_params(("arbitrary",)),
        name="flash_" + kind,
    )(qi, ki, *args)


def _sample_kernel(ixq_ref, wb_ref, ikT_ref, aq_ref, kc_ref, vc_ref, kn_ref, vn_ref, bias_ref, qc_ref,
                   kvb_ref, krp_ref, oa_ref, ob_ref, *, n_new, s_valid, past, topk):
    sp = ikT_ref.shape[2]
    qpos = past + lax.broadcasted_iota(I32, (n_new, sp), 0)
    kpos = lax.broadcasted_iota(I32, (n_new, sp), 1)
    adm = (kpos < s_valid) & ((kpos >> CHUNK_SHIFT) <= (qpos >> CHUNK_SHIFT))

    qb = ixq_ref[...].astype(BF16)
    w = wb_ref[...]
    kT = ikT_ref[0]
    off = MISC_IXW - MISC_KRSW
    score = jnp.zeros((n_new, sp), F32)
    for h in range(IDX_HEADS):
        d = jnp.dot(qb[:, h * IDX_DIM:(h + 1) * IDX_DIM], kT, preferred_element_type=F32)
        score = score + jnp.maximum(d, 0.0) * (w[:, off + h:off + h + 1] * (IDX_HEADS ** -0.5))
    key = jnp.where(adm, _sort_key(score), KEY_MASKED)

    kf = float(topk)

    def count_ge(cand):
        return jnp.sum(jnp.where(key >= cand, 1.0, 0.0), axis=1, keepdims=True)

    ans = jnp.where(count_ge(jnp.zeros((n_new, 1), I32)) >= kf, 0, INT_MIN).astype(I32)

    def bit_step(b, ans):
        cand = ans + jnp.left_shift(jnp.int32(1), 30 - b)
        return jnp.where(count_ge(cand) >= kf, cand, ans)

    ans = lax.fori_loop(0, 31, bit_step, ans)
    need = kf - count_ge(ans + 1)
    eq = jnp.where(key == ans, 1.0, 0.0)
    jj = lax.broadcasted_iota(I32, (sp, sp), 0)
    ii = lax.broadcasted_iota(I32, (sp, sp), 1)
    before = jnp.where(jj < ii, 1.0, 0.0).astype(BF16)
    rank = jnp.dot(eq.astype(BF16), before, preferred_element_type=F32)
    sel = ((key > ans) | ((eq > 0.5) & (rank < need))) & (key > KEY_HALF)

    def attend(s, vmat):
        m = jnp.max(s, axis=1, keepdims=True)
        pr = jnp.exp2(s - m)
        l = jnp.sum(pr, axis=1, keepdims=True)
        return jnp.dot(pr.astype(BF16), vmat, preferred_element_type=F32) / l

    row_pad = jnp.zeros((sp - past - n_new, LANES), BF16)
    for h in range(N_HEADS):
        hs = slice(h * LANES, (h + 1) * LANES)
        kn = jnp.concatenate([kn_ref[h], row_pad], axis=0)
        vn = jnp.concatenate([vn_ref[h], row_pad], axis=0)
        q = aq_ref[h]
        s = jnp.concatenate([lax.dot_general(q, kc_ref[0, h], _NT, preferred_element_type=F32),
                             lax.dot_general(q, kn, _NT, preferred_element_type=F32)], axis=1)
        s = jnp.where(sel, s + bias_ref[h], NEG_INF)
        m = jnp.max(s, axis=1, keepdims=True)
        pr = jnp.exp2(s - m)
        l = jnp.sum(pr, axis=1, keepdims=True)
        pr = pr.astype(BF16)
        pv = (jnp.dot(pr[:, :past], vc_ref[0, h], preferred_element_type=F32)
              + jnp.dot(pr[:, past:], vn, preferred_element_type=F32))
        oa_ref[:, hs] = pv / l

    krp = krp_ref[0]
    hd = N_HEADS * LANES
    for h in range(N_HEADS):
        hs = slice(h * LANES, (h + 1) * LANES)
        k = jnp.concatenate([kvb_ref[0, :, hs], krp], axis=1)
        s = lax.dot_general(qc_ref[h], k, _NT, preferred_element_type=F32)
        s = jnp.where(adm, s, NEG_INF)
        ob_ref[:, hs] = attend(s, kvb_ref[0, :, hd + h * LANES:hd + (h + 1) * LANES])


def _sample_attn(zmisc, ixk_t, aq_hm, kc, vc, kn_hm, vn_hm, bias, qcat, kvb, krp, n_new, s_valid, past,
                 topk):
    nb = kc.shape[0]
    hd = N_HEADS * LANES
    assert past % LANES == 0 and n_new <= LANES and ixk_t.shape[2] == past + LANES
    new_rows = pl.BlockSpec((N_HEADS, n_new, LANES), lambda b: (0, b, 0))
    per_stream = lambda a: pl.BlockSpec((1,) + a.shape[1:], lambda b: (b,) + (0,) * (a.ndim - 1))
    return pl.pallas_call(
        functools.partial(_sample_kernel, n_new=n_new, s_valid=s_valid, past=past, topk=topk),
        grid=(nb,),
        in_specs=[pl.BlockSpec((n_new, IDX_HEADS * IDX_DIM), lambda b: (b, 0)),
                  pl.BlockSpec((n_new, LANES), lambda b: (b, MISC_KRSW // LANES)),
                  per_stream(ixk_t),
                  new_rows, per_stream(kc), per_stream(vc), new_rows, new_rows,
                  pl.BlockSpec(bias.shape, lambda b: (0, 0, 0)),
                  pl.BlockSpec((N_HEADS, n_new, Q_CAT), lambda b: (0, b, 0)),
                  per_stream(kvb), per_stream(krp)],
        out_specs=[pl.BlockSpec((n_new, hd), lambda b: (b, 0)),
                   pl.BlockSpec((n_new, hd), lambda b: (b, 0))],
        out_shape=[jax.ShapeDtypeStruct((nb * n_new, hd), F32),
                   jax.ShapeDtypeStruct((nb * n_new, hd), F32)],
        compiler_params=_params(("parallel",)),
        name="sample_attn",
    )(zmisc, zmisc, ixk_t, aq_hm, kc, vc, kn_hm, vn_hm, bias, qcat, kvb, krp)


def _outproj_kernel(x_ref, ga_ref, gb_ref, oa_ref, ob_ref, w_ref, o_ref):
    mix = ga_ref[...] * oa_ref[...] + gb_ref[...] * ob_ref[...]
    o_ref[...] = x_ref[...] + jnp.dot(mix.astype(BF16), w_ref[...], preferred_element_type=F32)


def _outproj(x, ga, gb, oa, ob, w, tm):
    m, d = x.shape
    row = pl.BlockSpec((tm, d), lambda i: (i, 0))
    return pl.pallas_call(
        _outproj_kernel,
        grid=(m // tm,),
        in_specs=[row, row, row, row, row, pl.BlockSpec(w.shape, lambda i: (0, 0))],
        out_specs=row,
        out_shape=jax.ShapeDtypeStruct((m, d), F32),
        compiler_params=_params(("parallel",)),
        name="outproj",
    )(x, ga, gb, oa, ob, w)


def _ffn_kernel(x_ref, g_ref, wu_ref, wd_ref, gf_ref, o_ref, h_sc, acc_sc):
    f = pl.program_id(1)

    @pl.when(f == 0)
    def _start():
        h_sc[...] = _rms(x_ref[...], g_ref[...]).astype(BF16)
        acc_sc[...] = jnp.zeros(acc_sc.shape, F32)

    u = jnp.dot(h_sc[...], wu_ref[...], preferred_element_type=F32)
    u = jnp.square(jnp.maximum(u, 0.0)).astype(BF16)
    acc_sc[...] += jnp.dot(u, wd_ref[...], preferred_element_type=F32)

    @pl.when(f == pl.num_programs(1) - 1)
    def _end():
        o_ref[...] = _rms(x_ref[...] + acc_sc[...], gf_ref[...])


def _ffn(x, g, wu, wd, gf, tm, tf=1024):
    m, d = x.shape
    dff = wu.shape[1]
    return pl.pallas_call(
        _ffn_kernel,
        grid=(m // tm, dff // tf),
        in_specs=[pl.BlockSpec((tm, d), lambda i, f: (i, 0)),
                  pl.BlockSpec((1, d), lambda i, f: (0, 0)),
                  pl.BlockSpec((d, tf), lambda i, f: (0, f)),
                  pl.BlockSpec((tf, d), lambda i, f: (f, 0)),
                  pl.BlockSpec((1, d), lambda i, f: (0, 0))],
        out_specs=pl.BlockSpec((tm, d), lambda i, f: (i, 0)),
        out_shape=jax.ShapeDtypeStruct((m, d), F32),
        scratch_shapes=[pltpu.VMEM((tm, d), BF16), pltpu.VMEM((tm, d), F32)],
        compiler_params=_params(("parallel", "arbitrary")),
        name="ffn",
    )(x, g.reshape(1, d), wu, wd, gf.reshape(1, d))


def _t5_bucket(rel):
    nb = REL_BUCKETS // 2
    ret = (rel > 0).astype(jnp.int32) * nb
    n = jnp.abs(rel)
    max_exact = nb // 2
    large = max_exact + (jnp.log(jnp.maximum(n, 1).astype(jnp.float32) / max_exact)
                         / math.log(REL_MAX_DIST / max_exact) * (nb - max_exact)).astype(jnp.int32)
    large = jnp.minimum(large, nb - 1)
    return ret + jnp.where(n < max_exact, n, large)


def _rel_bias(table, rel):
    onehot = jax.nn.one_hot(_t5_bucket(rel), REL_BUCKETS, dtype=F32)
    return jnp.einsum("...b,bh->...h", onehot, table.astype(F32), precision=lax.Precision.HIGHEST)


def _rope_tables(pos):
    half = ROPE_DIM // 2
    inv_freq = jnp.power(ROPE_THETA, -jnp.arange(half, dtype=jnp.float32) / half)
    ang = pos.astype(jnp.float32)[:, None] * inv_freq[None, :]
    cos, sin = jnp.cos(ang), jnp.sin(ang)
    pad = jnp.zeros((pos.shape[0], LANES - ROPE_DIM), F32)
    return (jnp.concatenate([cos, cos, pad], axis=1), jnp.concatenate([-sin, sin, pad], axis=1))


def _swap_halves(w):
    half = ROPE_DIM // 2
    return jnp.concatenate([w[..., half:], w[..., :half]], axis=-1)


def _prep_weights(w_in, w_uq, w_uk, w_uv, w_out, w_ff_up, w_ff_down):
    d = w_in.shape[0]
    hd = N_HEADS * HEAD_DIM
    sizes = (hd, hd, hd, IDX_HEADS * IDX_DIM, IDX_DIM, IDX_HEADS, Q_LORA, KV_LORA, ROPE_DIM, d, d)
    cols, off = [], 0
    for n in sizes:
        cols.append(w_in[:, off:off + n])
        off += n
    wq, wk, wv, wixq, wixk, wixw, wcq, wckv, wkr, wga, wgb = cols
    pad = jnp.zeros((d, MISC_COLS - MISC_IXW - IDX_HEADS), w_in.dtype)
    wmisc = jnp.concatenate([wixq, wcq, wckv, wkr, wixk, _swap_halves(wkr), wixw, pad], axis=1)
    nope, rope = w_uq[..., :QK_NOPE_DIM], w_uq[..., QK_NOPE_DIM:]
    zpad = jnp.zeros(rope.shape, w_uq.dtype)
    wq_cat = jnp.concatenate([nope, rope, zpad], axis=-1).reshape(Q_LORA, N_HEADS * Q_CAT)
    wq_sw = jnp.concatenate([_swap_halves(rope), zpad], axis=-1).reshape(Q_LORA, N_HEADS * LANES)
    wkv = jnp.concatenate([w_uk.reshape(KV_LORA, hd), w_uv.reshape(KV_LORA, hd)], axis=1)
    c = lambda a: a.astype(BF16)
    return dict(q=c(wq), k=c(wk), v=c(wv), misc=c(wmisc), ga=c(wga), gb=c(wgb), wq_cat=c(wq_cat),
                wq_sw=c(wq_sw), wkv=c(wkv), out=c(w_out), up=c(w_ff_up), down=c(w_ff_down))


def _front(x, pos, w, g_mix, g_q, g_kv, tm):
    h = _norm_cast(x, g_mix, tm)
    aq_hm = _mm(h, w["q"], "heads", tm, head_scale=A_SCALE * LOG2E)
    ak, ak_hm = _mm(h, w["k"], "f32+heads", tm)
    av, av_hm = _mm(h, w["v"], "f32+heads", tm)
    zmisc = _mm(h, w["misc"], "f32", tm)
    sga = _mm(h, w["ga"], "sigmoid", tm)
    sgb = _mm(h, w["gb"], "sigmoid", tm)
    cos_t, sin_t = _rope_tables(pos)
    qcat, ckv, ckv_b, kr, krp = _mla_prep(zmisc, cos_t, sin_t, g_q, g_kv, w["wq_cat"], w["wq_sw"], tm)
    return dict(aq_hm=aq_hm, ak=ak, ak_hm=ak_hm, av=av, av_hm=av_hm, zmisc=zmisc, sga=sga, sgb=sgb,
                qcat=qcat, ckv=ckv, ckv_b=ckv_b, kr=kr, krp=krp)


def _back(x, f, oa, ob, w, g_ffn, g_final, tm_out, tm_ffn):
    x1 = _outproj(x, f["sga"], f["sgb"], oa, ob, w["out"], tm_out)
    return _ffn(x1, g_ffn, w["up"], w["down"], g_final, tm_ffn)


def kernel(x_prompt, x_sample, cache_a_k, cache_a_v, cache_a_idx_k, cache_b_ckv, cache_b_krope,
           rel_bias_table, norm_mix_g, w_in, q_lora_g, w_uq, kv_lora_g, w_uk, w_uv, w_out,
           norm_ffn_g, w_ff_up, w_ff_down, final_norm_g):
    assert w_in.shape[0] == 1, "single-layer trunk"
    _, seq, d = x_prompt.shape
    nb, n_new, _ = x_sample.shape
    past = cache_a_k.shape[2]
    s_valid = past + n_new
    sp = -(-s_valid // LANES) * LANES
    topk_p = min(TOPK_MAX, seq // 4)
    topk_s = min(TOPK_MAX, s_valid // 4)
    hd = N_HEADS * HEAD_DIM

    w = _prep_weights(w_in[0], w_uq[0], w_uk[0], w_uv[0], w_out[0], w_ff_up[0], w_ff_down[0])

    xp = x_prompt.reshape(seq, d)
    fp = _front(xp, jnp.arange(seq, dtype=jnp.int32), w, norm_mix_g[0], q_lora_g[0], kv_lora_g[0], 512)
    ixk_p = fp["zmisc"][:, MISC_IXK:MISC_IXK + IDX_DIM]
    mask = _indexer(fp["zmisc"], ixk_p.astype(BF16).T, topk_p)
    ii = jnp.arange(LANES, dtype=jnp.int32)
    rel0 = ii[None, :] - ii[:, None]
    far = _rel_bias(rel_bias_table, jnp.full((1, 1), -REL_MAX_DIST, jnp.int32))
    btiles = jnp.stack([_rel_bias(rel_bias_table, rel0) - far,
                        _rel_bias(rel_bias_table, rel0 - LANES) - far,
                        jnp.zeros((LANES, LANES, N_HEADS), F32)])
    btiles = jnp.transpose(btiles, (0, 3, 1, 2)).astype(F32) * LOG2E
    oa_p = _flash("dsa", fp["aq_hm"], fp["ak_hm"], fp["av_hm"], (mask, btiles), seq)
    kvb_p = _mm(fp["ckv_b"], w["wkv"], "heads", 512)
    ob_p = _flash("mla", fp["qcat"], kvb_p, kvb_p, (fp["krp"],), seq)
    y_p = _back(xp, fp, oa_p, ob_p, w, norm_ffn_g[0], final_norm_g, 256, 512)

    m_s = nb * n_new
    xs = x_sample.reshape(m_s, d)
    pos_s = jnp.tile(past + jnp.arange(n_new, dtype=jnp.int32), nb)
    fs = _front(xs, pos_s, w, norm_mix_g[0], q_lora_g[0], kv_lora_g[0], m_s)
    ixk_s = fs["zmisc"][:, MISC_IXK:MISC_IXK + IDX_DIM]

    def with_cache(cache, new, width):
        a = jnp.concatenate([cache.reshape(nb, past, width).astype(BF16),
                             new.reshape(nb, n_new, width).astype(BF16)], axis=1)
        return jnp.pad(a, ((0, 0), (0, sp - s_valid), (0, 0)))

    kc = jnp.transpose(cache_a_k[0].astype(BF16), (0, 2, 1, 3))
    vc = jnp.transpose(cache_a_v[0].astype(BF16), (0, 2, 1, 3))
    ixk_all = jnp.swapaxes(with_cache(cache_a_idx_k[0], ixk_s, IDX_DIM), 1, 2)
    ckv_all = with_cache(cache_b_ckv[0], fs["ckv"], KV_LORA)
    krope_pad = jnp.pad(cache_b_krope[0], ((0, 0), (0, 0), (0, LANES - ROPE_DIM)))
    krp_all = with_cache(krope_pad, fs["krp"], LANES)
    kvb_s = _mm(ckv_all.reshape(nb * sp, KV_LORA), w["wkv"], "bf16", sp).reshape(nb, sp, 2 * hd)
    qpos = past + jnp.arange(n_new, dtype=jnp.int32)
    kpos = jnp.arange(sp, dtype=jnp.int32)
    bias_s = jnp.transpose(_rel_bias(rel_bias_table, kpos[None, :] - qpos[:, None]), (2, 0, 1))
    oa_s, ob_s = _sample_attn(fs["zmisc"], ixk_all, fs["aq_hm"], kc, vc, fs["ak_hm"], fs["av_hm"],
                              bias_s.astype(F32) * LOG2E,
                              fs["qcat"], kvb_s, krp_all, n_new, s_valid, past, topk_s)
    y_s = _back(xs, fs, oa_s, ob_s, w, norm_ffn_g[0], final_norm_g, m_s, m_s)

    st = lambda a, b, *tail: a.reshape((1, b, -1) + tail)
    return (y_p.reshape(1, seq, d), y_s.reshape(nb, n_new, d),
            st(fp["ak"], 1, N_HEADS, HEAD_DIM), st(fp["av"], 1, N_HEADS, HEAD_DIM),
            st(ixk_p, 1, IDX_DIM), st(fp["ckv"], 1, KV_LORA), st(fp["kr"], 1, ROPE_DIM),
            st(fs["ak"], nb, N_HEADS, HEAD_DIM), st(fs["av"], nb, N_HEADS, HEAD_DIM),
            st(ixk_s, nb, IDX_DIM), st(fs["ckv"], nb, KV_LORA), st(fs["kr"], nb, ROPE_DIM))
```

```python
import functools
import math

import numpy as np
import jax
import jax.numpy as jnp
from jax import lax
from jax.experimental import pallas as pl
from jax.experimental.pallas import tpu as pltpu

F32 = jnp.float32
BF16 = jnp.bfloat16
I32 = jnp.int32

CHUNK = 64
CHUNK_SHIFT = 6
N_HEADS = 16
HEAD_DIM = 128
IDX_HEADS = 16
IDX_DIM = 64
TOPK_MAX = 256
REL_BUCKETS = 32
REL_MAX_DIST = 128
QK_NOPE_DIM = 128
ROPE_DIM = 64
Q_LORA = 512
KV_LORA = 256
ROPE_THETA = 10000.0
MLA_SCALE = (QK_NOPE_DIM + ROPE_DIM) ** -0.5
A_SCALE = HEAD_DIM ** -0.5
EPS = 1e-6
NEG_INF = -1e30
LOG2E = math.log2(math.e)

LANES = 128
Q_CAT = 2 * LANES
V7X_VMEM_LIMIT = 56 * 1024 * 1024

MISC_IXQ = 0
MISC_CQ = 1024
MISC_CKV = 1536
MISC_KR = 1792
MISC_IXK = 1856
MISC_KRSW = 1920
MISC_IXW = 1984
MISC_COLS = 2048


def _sortable_key_const(v):
    bits = int(np.float32(v).view(np.int32))
    return bits ^ ((bits >> 31) & 0x7FFFFFFF)


KEY_MASKED = _sortable_key_const(NEG_INF)
KEY_HALF = _sortable_key_const(np.float32(0.5) * np.float32(NEG_INF))
INT_MIN = -(2 ** 31)


def _params(sem):
    return pltpu.CompilerParams(dimension_semantics=sem, vmem_limit_bytes=V7X_VMEM_LIMIT)


def _rms(x, g):
    ms = jnp.mean(x * x, axis=-1, keepdims=True)
    return (x * lax.rsqrt(ms + EPS)) * g


def _sort_key(x):
    bits = lax.bitcast_convert_type(x, I32)
    return bits ^ ((bits >> 31) & 0x7FFFFFFF)


def _lane_tile(x, n):
    return x if n == 1 else jnp.concatenate([x] * n, axis=1)


def _norm_cast_kernel(x_ref, g_ref, o_ref):
    o_ref[...] = _rms(x_ref[...], g_ref[...]).astype(o_ref.dtype)


def _norm_cast(x, g, tm):
    m, d = x.shape
    return pl.pallas_call(
        _norm_cast_kernel,
        grid=(m // tm,),
        in_specs=[pl.BlockSpec((tm, d), lambda i: (i, 0)),
                  pl.BlockSpec((1, d), lambda i: (0, 0))],
        out_specs=pl.BlockSpec((tm, d), lambda i: (i, 0)),
        out_shape=jax.ShapeDtypeStruct((m, d), BF16),
        compiler_params=_params(("parallel",)),
        name="norm_cast",
    )(x, g.reshape(1, d))


def _mm_kernel(x_ref, w_ref, *o_refs, mode, head_scale):
    acc = jnp.dot(x_ref[...], w_ref[...], preferred_element_type=F32)
    if mode == "f32":
        o_refs[0][...] = acc
    elif mode == "sigmoid":
        o_refs[0][...] = jax.nn.sigmoid(acc)
    elif mode == "bf16":
        o_refs[0][...] = acc.astype(BF16)
    else:
        if mode == "f32+heads":
            o_refs[0][...] = acc
        hm = o_refs[-1]
        for h in range(hm.shape[0]):
            blk = acc[:, h * LANES:(h + 1) * LANES]
            hm[h] = (blk if head_scale == 1.0 else blk * head_scale).astype(BF16)


def _mm(x, w, mode, tm, head_scale=1.0):
    m, k = x.shape
    n = w.shape[1]
    flat = pl.BlockSpec((tm, n), lambda i: (i, 0))
    heads = pl.BlockSpec((n // LANES, tm, LANES), lambda i: (0, i, 0))
    hm_shape = jax.ShapeDtypeStruct((n // LANES, m, LANES), BF16)
    if mode in ("f32", "sigmoid"):
        out_specs, out_shape = flat, jax.ShapeDtypeStruct((m, n), F32)
    elif mode == "bf16":
        out_specs, out_shape = flat, jax.ShapeDtypeStruct((m, n), BF16)
    elif mode == "heads":
        out_specs, out_shape = heads, hm_shape
    else:
        out_specs = [flat, heads]
        out_shape = [jax.ShapeDtypeStruct((m, n), F32), hm_shape]
    return pl.pallas_call(
        functools.partial(_mm_kernel, mode=mode, head_scale=head_scale),
        grid=(m // tm,),
        in_specs=[pl.BlockSpec((tm, k), lambda i: (i, 0)),
                  pl.BlockSpec((k, n), lambda i: (0, 0))],
        out_specs=out_specs,
        out_shape=out_shape,
        compiler_params=_params(("parallel",)),
        name="mm_" + mode.replace("+", "_"),
    )(x, w)


def _mla_prep_kernel(cq_ref, ckv_ref, a_ref, b_ref, cos_ref, sin_ref, gq_ref, gkv_ref,
                     wq_ref, wsw_ref, qcat_ref, ckv_o, ckvb_o, kr_o, krp_o):
    cq = _rms(cq_ref[...], gq_ref[...]).astype(BF16)
    ckv = _rms(ckv_ref[...], gkv_ref[...])
    ckv_o[...] = ckv
    ckvb_o[...] = ckv.astype(BF16)
    cos = cos_ref[...]
    sin = sin_ref[...]
    r = a_ref[...] * cos + b_ref[...] * sin
    kr_o[...] = r[:, :ROPE_DIM]
    krp_o[...] = r.astype(BF16)
    for h in range(N_HEADS):
        qh = jnp.dot(cq, wq_ref[:, h * Q_CAT:(h + 1) * Q_CAT], preferred_element_type=F32)
        qs = jnp.dot(cq, wsw_ref[:, h * LANES:(h + 1) * LANES], preferred_element_type=F32)
        rp = qh[:, LANES:] * cos + qs * sin
        qcat = jnp.concatenate([qh[:, :LANES], rp], axis=1) * (MLA_SCALE * LOG2E)
        qcat_ref[h] = qcat.astype(BF16)


def _mla_prep(zmisc, cos_t, sin_t, gq, gkv, wq_cat, wq_sw, tm):
    m = zmisc.shape[0]
    row = lambda w, j: pl.BlockSpec((tm, w), lambda i, j=j: (i, j))
    const = lambda a: pl.BlockSpec(a.shape, lambda i: (0,) * a.ndim)
    gq2, gkv2 = gq.reshape(1, Q_LORA), gkv.reshape(1, KV_LORA)
    return pl.pallas_call(
        _mla_prep_kernel,
        grid=(m // tm,),
        in_specs=[row(Q_LORA, MISC_CQ // Q_LORA), row(KV_LORA, MISC_CKV // KV_LORA),
                  row(LANES, MISC_KR // LANES), row(LANES, MISC_KRSW // LANES),
                  row(LANES, 0), row(LANES, 0), const(gq2), const(gkv2),
                  const(wq_cat), const(wq_sw)],
        out_specs=[pl.BlockSpec((N_HEADS, tm, Q_CAT), lambda i: (0, i, 0)),
                   row(KV_LORA, 0), row(KV_LORA, 0), row(ROPE_DIM, 0), row(LANES, 0)],
        out_shape=[jax.ShapeDtypeStruct((N_HEADS, m, Q_CAT), BF16),
                   jax.ShapeDtypeStruct((m, KV_LORA), F32),
                   jax.ShapeDtypeStruct((m, KV_LORA), BF16),
                   jax.ShapeDtypeStruct((m, ROPE_DIM), F32),
                   jax.ShapeDtypeStruct((m, LANES), BF16)],
        compiler_params=_params(("parallel",)),
        name="mla_prep",
    )(zmisc, zmisc, zmisc, zmisc, cos_t, sin_t, gq2, gkv2, wq_cat, wq_sw)


SCORE_TILE = 256


def _indexer_kernel(q_ref, wb_ref, kT_ref, mask_ref, key_sc, w_sc, q_sc, *, tq, ts, cw, topk):
    i = pl.program_id(0)
    s_total = kT_ref.shape[2]
    t0 = i * tq
    front = t0 + tq
    nblk = (front + ts - 1) // ts
    nch = nblk * (ts // cw)

    w = wb_ref[...]
    off = MISC_IXW - MISC_KRSW
    for h in range(IDX_HEADS):
        w_sc[h] = jnp.broadcast_to(w[:, off + h:off + h + 1] * (IDX_HEADS ** -0.5), (tq, LANES))
    qb = q_ref[...].astype(BF16)
    npair = IDX_HEADS // 2
    for j in range(npair):
        q_sc[j * tq:(j + 1) * tq, :] = qb[:, j * LANES:(j + 1) * LANES]

    def score_block(kb, rmax):
        c0 = pl.multiple_of(kb * ts, ts)
        for c in range(ts // SCORE_TILE):
            cc = pl.multiple_of(c0 + c * SCORE_TILE, SCORE_TILE)
            w_both = jnp.concatenate([kT_ref[0, :, pl.ds(cc, SCORE_TILE)],
                                      kT_ref[1, :, pl.ds(cc, SCORE_TILE)]], axis=1)
            d_both = jnp.dot(q_sc[...], w_both, preferred_element_type=F32)
            acc = jnp.zeros((tq, SCORE_TILE), F32)
            for h in range(IDX_HEADS):
                d = d_both[(h // 2) * tq:(h // 2 + 1) * tq, (h % 2) * SCORE_TILE:(h % 2 + 1) * SCORE_TILE]
                acc = acc + jnp.maximum(d, 0.0) * _lane_tile(w_sc[h], SCORE_TILE // LANES)
            rows = t0 + lax.broadcasted_iota(I32, (tq, SCORE_TILE), 0)
            cols = cc + lax.broadcasted_iota(I32, (tq, SCORE_TILE), 1)
            adm = (cols >> CHUNK_SHIFT) <= (rows >> CHUNK_SHIFT)
            sc = jnp.where(adm, acc, NEG_INF)
            bits = lax.bitcast_convert_type(sc, I32)
            key_sc[:, pl.ds(cc, SCORE_TILE)] = bits ^ ((bits >> 31) & 0x7FFFFFFF)
            for j in range(SCORE_TILE // LANES):
                rmax = jnp.maximum(rmax, sc[:, j * LANES:(j + 1) * LANES])
        return rmax

    rmax = lax.fori_loop(0, nblk, score_block, jnp.full((tq, LANES), NEG_INF, F32))

    kf = float(topk)

    def count_ge(cand):
        def body(c, cnt):
            c0 = pl.multiple_of(c * cw, cw)
            k = key_sc[:, pl.ds(c0, cw)]
            for j in range(cw // LANES):
                cnt = cnt + jnp.where(k[:, j * LANES:(j + 1) * LANES] >= cand, 1.0, 0.0)
            return cnt
        cnt = lax.fori_loop(0, nch, body, jnp.zeros((tq, LANES), F32))
        return jnp.broadcast_to(jnp.sum(cnt, axis=1, keepdims=True), (tq, LANES))

    rows1 = t0 + lax.broadcasted_iota(I32, (tq, LANES), 0)
    few = (((rows1 >> CHUNK_SHIFT) + 1) << CHUNK_SHIFT) < topk
    row_max = jnp.broadcast_to(jnp.max(rmax, axis=1, keepdims=True), (tq, LANES))
    lo = jnp.full((tq, LANES), KEY_MASKED, I32)
    hi = _sort_key(row_max) + 1
    cnt_lo = jnp.broadcast_to((nch * cw).astype(F32), (tq, LANES))
    cnt_hi = jnp.zeros((tq, LANES), F32)
    log_k = math.log(topk)

    def key_value(k):
        return lax.bitcast_convert_type(k ^ ((k >> 31) & 0x7FFFFFFF), F32)

    def done_of(lo, hi, cnt_lo):
        return few | (cnt_lo == kf) | (hi - 1 <= lo)

    def search_cond(st):
        return st[5] > 0.5

    def search_step(st):
        lo, hi, cnt_lo, cnt_hi, step, _ = st
        done = done_of(lo, hi, cnt_lo)
        mid = (lo >> 1) + (hi >> 1) + (lo & hi & 1)
        lo_v, hi_v = key_value(lo), key_value(hi)
        g_lo = jnp.log(cnt_lo)
        g_hi = jnp.log(jnp.maximum(cnt_hi, 0.5))
        frac = (g_lo - log_k) / (g_lo - g_hi)
        guess = _sort_key(lo_v + frac * (hi_v - lo_v))
        use_mid = (lo <= KEY_HALF) | (step % 3 == 2)
        first = jnp.zeros((tq, LANES), I32)
        cand = jnp.where(step == 0, first, jnp.where(use_mid, mid, guess))
        cand = jnp.minimum(jnp.maximum(cand, lo + 1), hi - 1)
        cnt = count_ge(cand)
        up = (cnt >= kf) & jnp.logical_not(done)
        down = (cnt < kf) & jnp.logical_not(done)
        lo = jnp.where(up, cand, lo)
        cnt_lo = jnp.where(up, cnt, cnt_lo)
        hi = jnp.where(down, cand, hi)
        cnt_hi = jnp.where(down, cnt, cnt_hi)
        active = jnp.max(jnp.where(done_of(lo, hi, cnt_lo), 0.0, 1.0))
        return lo, hi, cnt_lo, cnt_hi, step + 1, active

    active0 = jnp.max(jnp.where(done_of(lo, hi, cnt_lo), 0.0, 1.0))
    lo, hi, cnt_lo, cnt_hi, _, _ = lax.while_loop(
        search_cond, search_step, (lo, hi, cnt_lo, cnt_hi, jnp.int32(0), active0))
    ans = jnp.where(few, INT_MIN, lo)
    thr = _lane_tile(jnp.maximum(ans, KEY_HALF + 1), cw // LANES)
    tied = jnp.logical_not(few) & (cnt_lo > kf) & (lo > KEY_HALF)
    any_tied = jnp.max(jnp.where(tied, 1.0, 0.0))

    def write_mask(c, carry):
        c0 = pl.multiple_of(c * cw, cw)
        sel = key_sc[:, pl.ds(c0, cw)] >= thr
        mask_ref[:, pl.ds(c0, cw)] = jnp.where(sel, 0.0, NEG_INF).astype(BF16)
        return carry

    @pl.when(any_tied < 0.5)
    def _plain():
        lax.fori_loop(0, nch, write_mask, 0)

    @pl.when(any_tied >= 0.5)
    def _with_ties():
        need = kf - count_ge(lo + 1)
        lo_t = _lane_tile(lo, cw // LANES)
        tied_t = _lane_tile(jnp.where(tied, 1.0, 0.0), cw // LANES) > 0.5
        need_t = _lane_tile(need, cw // LANES)
        jj = lax.broadcasted_iota(I32, (cw, cw), 0)
        ii = lax.broadcasted_iota(I32, (cw, cw), 1)
        upper = jnp.where(jj <= ii, 1.0, 0.0).astype(BF16)

        def write_tied(c, seen):
            c0 = pl.multiple_of(c * cw, cw)
            k = key_sc[:, pl.ds(c0, cw)]
            eq = jnp.where(k == lo_t, 1.0, 0.0)
            incl = jnp.dot(eq.astype(BF16), upper, preferred_element_type=F32)
            rank = _lane_tile(seen, cw // LANES) + incl - eq
            take = (k > lo_t) | ((eq > 0.5) & (rank < need_t))
            sel = (tied_t & take) | (jnp.logical_not(tied_t) & (k >= thr))
            mask_ref[:, pl.ds(c0, cw)] = jnp.where(sel, 0.0, NEG_INF).astype(BF16)
            return seen + jnp.broadcast_to(incl[:, cw - 1:cw], (tq, LANES))

        lax.fori_loop(0, nch, write_tied, jnp.zeros((tq, LANES), F32))

    def zero_mask(c, carry):
        c0 = pl.multiple_of(c * cw, cw)
        mask_ref[:, pl.ds(c0, cw)] = jnp.full((tq, cw), NEG_INF, BF16)
        return carry

    lax.fori_loop(nch, s_total // cw, zero_mask, 0)


def _indexer(zmisc, ixk_t, topk, tq=128, ts=1024, cw=512):
    s = ixk_t.shape[1]
    assert ts % cw == 0 and s % ts == 0 and ts % tq == 0
    zero = jnp.zeros_like(ixk_t)
    ixk_t = jnp.stack([jnp.concatenate([ixk_t, zero], axis=0), jnp.concatenate([zero, ixk_t], axis=0)])
    return pl.pallas_call(
        functools.partial(_indexer_kernel, tq=tq, ts=ts, cw=cw, topk=topk),
        grid=(s // tq,),
        in_specs=[pl.BlockSpec((tq, IDX_HEADS * IDX_DIM), lambda i: (i, 0)),
                  pl.BlockSpec((tq, LANES), lambda i: (i, MISC_KRSW // LANES)),
                  pl.BlockSpec((2, 2 * IDX_DIM, s), lambda i: (0, 0, 0))],
        out_specs=pl.BlockSpec((tq, s), lambda i: (i, 0)),
        out_shape=jax.ShapeDtypeStruct((s, s), BF16),
        scratch_shapes=[pltpu.VMEM((tq, s), I32),
                        pltpu.VMEM((IDX_HEADS, tq, LANES), F32),
                        pltpu.VMEM((IDX_HEADS // 2 * tq, 2 * IDX_DIM), BF16)],
        compiler_params=_params(("parallel",)),
        name="indexer",
    )(zmisc, zmisc, ixk_t)


_NT = (((1,), (1,)), ((), ()))
FLASH_AHEAD = 2
FLASH_GROUP = 16


def _flash_kernel(qi_ref, ki_ref, *refs, kind, tb):
    if kind == "dsa":
        q_ref, k_ref, v_ref, mask_ref, b_ref, o_ref, m_sc, l_sc, acc_sc, mb_sc = refs
    else:
        q_ref, k_ref, kr_ref, v_ref, o_ref, m_sc, l_sc, acc_sc, mb_sc = refs
    p = pl.program_id(0)
    qi = qi_ref[p]
    ki = ki_ref[p]
    nsub = tb // LANES

    @pl.when(ki == 0)
    def _init():
        m_sc[...] = jnp.full(m_sc.shape, NEG_INF, F32)
        l_sc[...] = jnp.zeros(l_sc.shape, F32)
        acc_sc[...] = jnp.zeros(acc_sc.shape, F32)

    ones = jnp.ones((tb, LANES), BF16)

    def update(h, s):
        m_prev = m_sc[h]
        m_new = jnp.maximum(m_prev, jnp.max(s, axis=1, keepdims=True))
        alpha = jnp.exp2(m_prev - m_new)
        pr = jnp.exp2(s - _lane_tile(m_new, nsub)).astype(BF16)
        pv = jnp.dot(pr, jnp.concatenate([v_ref[h], ones], axis=1), preferred_element_type=F32)
        acc_sc[h] = alpha * acc_sc[h] + pv[:, :LANES]
        l_sc[h] = alpha * l_sc[h] + pv[:, LANES:]
        m_sc[h] = m_new

    def run_heads(logits):
        def group(g, carry):
            base = g * FLASH_GROUP
            ahead = [logits(base + j) for j in range(min(FLASH_AHEAD, FLASH_GROUP))]
            for j in range(FLASH_GROUP):
                s = ahead.pop(0)
                if j + FLASH_AHEAD < FLASH_GROUP:
                    ahead.append(logits(base + j + FLASH_AHEAD))
                update(base + j, s)
            return carry

        lax.fori_loop(0, N_HEADS // FLASH_GROUP, group, 0)

    if kind == "dsa":
        mb_sc[...] = mask_ref[...].astype(F32)

        def logits_far(h):
            return lax.dot_general(q_ref[h], k_ref[h], _NT, preferred_element_type=F32) + mb_sc[...]

        on_diag = ki == qi
        i_main = jnp.where(on_diag, 0, 2)
        i_sub = jnp.where(on_diag, 1, 2)
        i_corner = jnp.where(on_diag, 2, 1)

        def logits_near(h):
            s = logits_far(h)
            rows = []
            for a in range(nsub):
                blocks = []
                for b in range(nsub):
                    blk = s[a * LANES:(a + 1) * LANES, b * LANES:(b + 1) * LANES]
                    if b == a:
                        blk = blk + b_ref[i_main, h]
                    elif b == a - 1:
                        blk = blk + b_ref[i_sub, h]
                    elif a == 0 and b == nsub - 1:
                        blk = blk + b_ref[i_corner, h]
                    blocks.append(blk)
                rows.append(jnp.concatenate(blocks, axis=1))
            return jnp.concatenate(rows, axis=0)

        @pl.when(ki < qi - 1)
        def _far():
            run_heads(logits_far)

        @pl.when(ki >= qi - 1)
        def _near():
            run_heads(logits_near)
    else:
        @pl.when(p == 0)
        def _masks():
            rows = lax.broadcasted_iota(I32, (tb, tb), 0)
            cols = lax.broadcasted_iota(I32, (tb, tb), 1)
            vis = (cols >> CHUNK_SHIFT) <= (rows >> CHUNK_SHIFT)
            mb_sc[...] = jnp.where(vis, 0.0, NEG_INF)

        def qk(h):
            k = jnp.concatenate([k_ref[h], kr_ref[...]], axis=1)
            return lax.dot_general(q_ref[h], k, _NT, preferred_element_type=F32)

        @pl.when(ki < qi)
        def _off():
            run_heads(qk)

        @pl.when(ki == qi)
        def _diag():
            run_heads(lambda h: qk(h) + mb_sc[...])

    @pl.when(ki == qi)
    def _finish():
        for h in range(N_HEADS):
            o_ref[:, h * LANES:(h + 1) * LANES] = acc_sc[h] / l_sc[h]


def _flash(kind, q, k, v, extra, seq, tb=512):
    nq = seq // tb
    pairs = [(a, b) for a in range(nq) for b in range(a + 1)]
    qi = jnp.asarray([a for a, _ in pairs], I32)
    ki = jnp.asarray([b for _, b in pairs], I32)
    dq = q.shape[2]
    qspec = pl.BlockSpec((N_HEADS, tb, dq), lambda p, qi, ki: (0, qi[p], 0))
    scratch = [pltpu.VMEM((N_HEADS, tb, LANES), F32), pltpu.VMEM((N_HEADS, tb, LANES), F32),
               pltpu.VMEM((N_HEADS, tb, LANES), F32)]
    if kind == "dsa":
        mask, btiles = extra
        in_specs = [qspec,
                    pl.BlockSpec((N_HEADS, tb, LANES), lambda p, qi, ki: (0, ki[p], 0)),
                    pl.BlockSpec((N_HEADS, tb, LANES), lambda p, qi, ki: (0, ki[p], 0)),
                    pl.BlockSpec((tb, tb), lambda p, qi, ki: (qi[p], ki[p])),
                    pl.BlockSpec(btiles.shape, lambda p, qi, ki: (0, 0, 0, 0))]
        args = (q, k, v, mask, btiles)
        scratch = scratch + [pltpu.VMEM((tb, tb), F32)]
    else:
        (krp,) = extra
        in_specs = [qspec,
                    pl.BlockSpec((N_HEADS, tb, LANES), lambda p, qi, ki: (0, ki[p], 0)),
                    pl.BlockSpec((tb, LANES), lambda p, qi, ki: (ki[p], 0)),
                    pl.BlockSpec((N_HEADS, tb, LANES), lambda p, qi, ki: (1, ki[p], 0))]
        args = (q, k, krp, v)
        scratch = scratch + [pltpu.VMEM((tb, tb), F32)]
    return pl.pallas_call(
        functools.partial(_flash_kernel, kind=kind, tb=tb),
        grid_spec=pltpu.PrefetchScalarGridSpec(
            num_scalar_prefetch=2,
            grid=(len(pairs),),
            in_specs=in_specs,
            out_specs=pl.BlockSpec((tb, N_HEADS * LANES), lambda p, qi, ki: (qi[p], 0)),
            scratch_shapes=scratch),
        out_shape=jax.ShapeDtypeStruct((seq, N_HEADS * LANES), F32),
        compiler_params=_params(("arbitrary",)),
        name="flash_" + kind,
    )(qi, ki, *args)


def _sample_kernel(ixq_ref, wb_ref, ikT_ref, aq_ref, kc_ref, vc_ref, kn_ref, vn_ref, bias_ref, qc_ref,
                   kvb_ref, krp_ref, oa_ref, ob_ref, *, n_new, s_valid, past, topk):
    sp = ikT_ref.shape[2]
    qpos = past + lax.broadcasted_iota(I32, (n_new, sp), 0)
    kpos = lax.broadcasted_iota(I32, (n_new, sp), 1)
    adm = (kpos < s_valid) & ((kpos >> CHUNK_SHIFT) <= (qpos >> CHUNK_SHIFT))

    qb = ixq_ref[...].astype(BF16)
    w = wb_ref[...]
    kT = ikT_ref[0]
    off = MISC_IXW - MISC_KRSW
    score = jnp.zeros((n_new, sp), F32)
    for h in range(IDX_HEADS):
        d = jnp.dot(qb[:, h * IDX_DIM:(h + 1) * IDX_DIM], kT, preferred_element_type=F32)
        score = score + jnp.maximum(d, 0.0) * (w[:, off + h:off + h + 1] * (IDX_HEADS ** -0.5))
    key = jnp.where(adm, _sort_key(score), KEY_MASKED)

    kf = float(topk)

    def count_ge(cand):
        return jnp.sum(jnp.where(key >= cand, 1.0, 0.0), axis=1, keepdims=True)

    ans = jnp.where(count_ge(jnp.zeros((n_new, 1), I32)) >= kf, 0, INT_MIN).astype(I32)

    def bit_step(b, ans):
        cand = ans + jnp.left_shift(jnp.int32(1), 30 - b)
        return jnp.where(count_ge(cand) >= kf, cand, ans)

    ans = lax.fori_loop(0, 31, bit_step, ans)
    need = kf - count_ge(ans + 1)
    eq = jnp.where(key == ans, 1.0, 0.0)
    jj = lax.broadcasted_iota(I32, (sp, sp), 0)
    ii = lax.broadcasted_iota(I32, (sp, sp), 1)
    before = jnp.where(jj < ii, 1.0, 0.0).astype(BF16)
    rank = jnp.dot(eq.astype(BF16), before, preferred_element_type=F32)
    sel = ((key > ans) | ((eq > 0.5) & (rank < need))) & (key > KEY_HALF)

    def attend(s, vmat):
        m = jnp.max(s, axis=1, keepdims=True)
        pr = jnp.exp2(s - m)
        l = jnp.sum(pr, axis=1, keepdims=True)
        return jnp.dot(pr.astype(BF16), vmat, preferred_element_type=F32) / l

    row_pad = jnp.zeros((sp - past - n_new, LANES), BF16)
    for h in range(N_HEADS):
        hs = slice(h * LANES, (h + 1) * LANES)
        kn = jnp.concatenate([kn_ref[h], row_pad], axis=0)
        vn = jnp.concatenate([vn_ref[h], row_pad], axis=0)
        q = aq_ref[h]
        s = jnp.concatenate([lax.dot_general(q, kc_ref[0, h], _NT, preferred_element_type=F32),
                             lax.dot_general(q, kn, _NT, preferred_element_type=F32)], axis=1)
        s = jnp.where(sel, s + bias_ref[h], NEG_INF)
        m = jnp.max(s, axis=1, keepdims=True)
        pr = jnp.exp2(s - m)
        l = jnp.sum(pr, axis=1, keepdims=True)
        pr = pr.astype(BF16)
        pv = (jnp.dot(pr[:, :past], vc_ref[0, h], preferred_element_type=F32)
              + jnp.dot(pr[:, past:], vn, preferred_element_type=F32))
        oa_ref[:, hs] = pv / l

    krp = krp_ref[0]
    hd = N_HEADS * LANES
    for h in range(N_HEADS):
        hs = slice(h * LANES, (h + 1) * LANES)
        k = jnp.concatenate([kvb_ref[0, :, hs], krp], axis=1)
        s = lax.dot_general(qc_ref[h], k, _NT, preferred_element_type=F32)
        s = jnp.where(adm, s, NEG_INF)
        ob_ref[:, hs] = attend(s, kvb_ref[0, :, hd + h * LANES:hd + (h + 1) * LANES])


def _sample_attn(zmisc, ixk_t, aq_hm, kc, vc, kn_hm, vn_hm, bias, qcat, kvb, krp, n_new, s_valid, past,
                 topk):
    nb = kc.shape[0]
    hd = N_HEADS * LANES
    assert past % LANES == 0 and n_new <= LANES and ixk_t.shape[2] == past + LANES
    new_rows = pl.BlockSpec((N_HEADS, n_new, LANES), lambda b: (0, b, 0))
    per_stream = lambda a: pl.BlockSpec((1,) + a.shape[1:], lambda b: (b,) + (0,) * (a.ndim - 1))
    return pl.pallas_call(
        functools.partial(_sample_kernel, n_new=n_new, s_valid=s_valid, past=past, topk=topk),
        grid=(nb,),
        in_specs=[pl.BlockSpec((n_new, IDX_HEADS * IDX_DIM), lambda b: (b, 0)),
                  pl.BlockSpec((n_new, LANES), lambda b: (b, MISC_KRSW // LANES)),
                  per_stream(ixk_t),
                  new_rows, per_stream(kc), per_stream(vc), new_rows, new_rows,
                  pl.BlockSpec(bias.shape, lambda b: (0, 0, 0)),
                  pl.BlockSpec((N_HEADS, n_new, Q_CAT), lambda b: (0, b, 0)),
                  per_stream(kvb), per_stream(krp)],
        out_specs=[pl.BlockSpec((n_new, hd), lambda b: (b, 0)),
                   pl.BlockSpec((n_new, hd), lambda b: (b, 0))],
        out_shape=[jax.ShapeDtypeStruct((nb * n_new, hd), F32),
                   jax.ShapeDtypeStruct((nb * n_new, hd), F32)],
        compiler_params=_params(("parallel",)),
        name="sample_attn",
    )(zmisc, zmisc, ixk_t, aq_hm, kc, vc, kn_hm, vn_hm, bias, qcat, kvb, krp)


def _outproj_kernel(x_ref, ga_ref, gb_ref, oa_ref, ob_ref, w_ref, o_ref):
    mix = ga_ref[...] * oa_ref[...] + gb_ref[...] * ob_ref[...]
    o_ref[...] = x_ref[...] + jnp.dot(mix.astype(BF16), w_ref[...], preferred_element_type=F32)


def _outproj(x, ga, gb, oa, ob, w, tm):
    m, d = x.shape
    row = pl.BlockSpec((tm, d), lambda i: (i, 0))
    return pl.pallas_call(
        _outproj_kernel,
        grid=(m // tm,),
        in_specs=[row, row, row, row, row, pl.BlockSpec(w.shape, lambda i: (0, 0))],
        out_specs=row,
        out_shape=jax.ShapeDtypeStruct((m, d), F32),
        compiler_params=_params(("parallel",)),
        name="outproj",
    )(x, ga, gb, oa, ob, w)


def _ffn_kernel(x_ref, g_ref, wu_ref, wd_ref, gf_ref, o_ref, h_sc, acc_sc):
    f = pl.program_id(1)

    @pl.when(f == 0)
    def _start():
        h_sc[...] = _rms(x_ref[...], g_ref[...]).astype(BF16)
        acc_sc[...] = jnp.zeros(acc_sc.shape, F32)

    u = jnp.dot(h_sc[...], wu_ref[...], preferred_element_type=F32)
    u = jnp.square(jnp.maximum(u, 0.0)).astype(BF16)
    acc_sc[...] += jnp.dot(u, wd_ref[...], preferred_element_type=F32)

    @pl.when(f == pl.num_programs(1) - 1)
    def _end():
        o_ref[...] = _rms(x_ref[...] + acc_sc[...], gf_ref[...])


def _ffn(x, g, wu, wd, gf, tm, tf=1024):
    m, d = x.shape
    dff = wu.shape[1]
    return pl.pallas_call(
        _ffn_kernel,
        grid=(m // tm, dff // tf),
        in_specs=[pl.BlockSpec((tm, d), lambda i, f: (i, 0)),
                  pl.BlockSpec((1, d), lambda i, f: (0, 0)),
                  pl.BlockSpec((d, tf), lambda i, f: (0, f)),
                  pl.BlockSpec((tf, d), lambda i, f: (f, 0)),
                  pl.BlockSpec((1, d), lambda i, f: (0, 0))],
        out_specs=pl.BlockSpec((tm, d), lambda i, f: (i, 0)),
        out_shape=jax.ShapeDtypeStruct((m, d), F32),
        scratch_shapes=[pltpu.VMEM((tm, d), BF16), pltpu.VMEM((tm, d), F32)],
        compiler_params=_params(("parallel", "arbitrary")),
        name="ffn",
    )(x, g.reshape(1, d), wu, wd, gf.reshape(1, d))


def _t5_bucket(rel):
    nb = REL_BUCKETS // 2
    ret = (rel > 0).astype(jnp.int32) * nb
    n = jnp.abs(rel)
    max_exact = nb // 2
    large = max_exact + (jnp.log(jnp.maximum(n, 1).astype(jnp.float32) / max_exact)
                         / math.log(REL_MAX_DIST / max_exact) * (nb - max_exact)).astype(jnp.int32)
    large = jnp.minimum(large, nb - 1)
    return ret + jnp.where(n < max_exact, n, large)


def _rel_bias(table, rel):
    onehot = jax.nn.one_hot(_t5_bucket(rel), REL_BUCKETS, dtype=F32)
    return jnp.einsum("...b,bh->...h", onehot, table.astype(F32), precision=lax.Precision.HIGHEST)


def _rope_tables(pos):
    half = ROPE_DIM // 2
    inv_freq = jnp.power(ROPE_THETA, -jnp.arange(half, dtype=jnp.float32) / half)
    ang = pos.astype(jnp.float32)[:, None] * inv_freq[None, :]
    cos, sin = jnp.cos(ang), jnp.sin(ang)
    pad = jnp.zeros((pos.shape[0], LANES - ROPE_DIM), F32)
    return (jnp.concatenate([cos, cos, pad], axis=1), jnp.concatenate([-sin, sin, pad], axis=1))


def _swap_halves(w):
    half = ROPE_DIM // 2
    return jnp.concatenate([w[..., half:], w[..., :half]], axis=-1)


def _prep_weights(w_in, w_uq, w_uk, w_uv, w_out, w_ff_up, w_ff_down):
    d = w_in.shape[0]
    hd = N_HEADS * HEAD_DIM
    sizes = (hd, hd, hd, IDX_HEADS * IDX_DIM, IDX_DIM, IDX_HEADS, Q_LORA, KV_LORA, ROPE_DIM, d, d)
    cols, off = [], 0
    for n in sizes:
        cols.append(w_in[:, off:off + n])
        off += n
    wq, wk, wv, wixq, wixk, wixw, wcq, wckv, wkr, wga, wgb = cols
    pad = jnp.zeros((d, MISC_COLS - MISC_IXW - IDX_HEADS), w_in.dtype)
    wmisc = jnp.concatenate([wixq, wcq, wckv, wkr, wixk, _swap_halves(wkr), wixw, pad], axis=1)
    nope, rope = w_uq[..., :QK_NOPE_DIM], w_uq[..., QK_NOPE_DIM:]
    zpad = jnp.zeros(rope.shape, w_uq.dtype)
    wq_cat = jnp.concatenate([nope, rope, zpad], axis=-1).reshape(Q_LORA, N_HEADS * Q_CAT)
    wq_sw = jnp.concatenate([_swap_halves(rope), zpad], axis=-1).reshape(Q_LORA, N_HEADS * LANES)
    wkv = jnp.concatenate([w_uk.reshape(KV_LORA, hd), w_uv.reshape(KV_LORA, hd)], axis=1)
    c = lambda a: a.astype(BF16)
    return dict(q=c(wq), k=c(wk), v=c(wv), misc=c(wmisc), ga=c(wga), gb=c(wgb), wq_cat=c(wq_cat),
                wq_sw=c(wq_sw), wkv=c(wkv), out=c(w_out), up=c(w_ff_up), down=c(w_ff_down))


def _front(x, pos, w, g_mix, g_q, g_kv, tm):
    h = _norm_cast(x, g_mix, tm)
    aq_hm = _mm(h, w["q"], "heads", tm, head_scale=A_SCALE * LOG2E)
    ak, ak_hm = _mm(h, w["k"], "f32+heads", tm)
    av, av_hm = _mm(h, w["v"], "f32+heads", tm)
    zmisc = _mm(h, w["misc"], "f32", tm)
    sga = _mm(h, w["ga"], "sigmoid", tm)
    sgb = _mm(h, w["gb"], "sigmoid", tm)
    cos_t, sin_t = _rope_tables(pos)
    qcat, ckv, ckv_b, kr, krp = _mla_prep(zmisc, cos_t, sin_t, g_q, g_kv, w["wq_cat"], w["wq_sw"], tm)
    return dict(aq_hm=aq_hm, ak=ak, ak_hm=ak_hm, av=av, av_hm=av_hm, zmisc=zmisc, sga=sga, sgb=sgb,
                qcat=qcat, ckv=ckv, ckv_b=ckv_b, kr=kr, krp=krp)


def _back(x, f, oa, ob, w, g_ffn, g_final, tm_out, tm_ffn):
    x1 = _outproj(x, f["sga"], f["sgb"], oa, ob, w["out"], tm_out)
    return _ffn(x1, g_ffn, w["up"], w["down"], g_final, tm_ffn)


def kernel(x_prompt, x_sample, cache_a_k, cache_a_v, cache_a_idx_k, cache_b_ckv, cache_b_krope,
           rel_bias_table, norm_mix_g, w_in, q_lora_g, w_uq, kv_lora_g, w_uk, w_uv, w_out,
           norm_ffn_g, w_ff_up, w_ff_down, final_norm_g):
    assert w_in.shape[0] == 1, "single-layer trunk"
    _, seq, d = x_prompt.shape
    nb, n_new, _ = x_sample.shape
    past = cache_a_k.shape[2]
    s_valid = past + n_new
    sp = -(-s_valid // LANES) * LANES
    topk_p = min(TOPK_MAX, seq // 4)
    topk_s = min(TOPK_MAX, s_valid // 4)
    hd = N_HEADS * HEAD_DIM

    w = _prep_weights(w_in[0], w_uq[0], w_uk[0], w_uv[0], w_out[0], w_ff_up[0], w_ff_down[0])

    xp = x_prompt.reshape(seq, d)
    fp = _front(xp, jnp.arange(seq, dtype=jnp.int32), w, norm_mix_g[0], q_lora_g[0], kv_lora_g[0], 512)
    ixk_p = fp["zmisc"][:, MISC_IXK:MISC_IXK + IDX_DIM]
    mask = _indexer(fp["zmisc"], ixk_p.astype(BF16).T, topk_p)
    ii = jnp.arange(LANES, dtype=jnp.int32)
    rel0 = ii[None, :] - ii[:, None]
    far = _rel_bias(rel_bias_table, jnp.full((1, 1), -REL_MAX_DIST, jnp.int32))
    btiles = jnp.stack([_rel_bias(rel_bias_table, rel0) - far,
                        _rel_bias(rel_bias_table, rel0 - LANES) - far,
                        jnp.zeros((LANES, LANES, N_HEADS), F32)])
    btiles = jnp.transpose(btiles, (0, 3, 1, 2)).astype(F32) * LOG2E
    oa_p = _flash("dsa", fp["aq_hm"], fp["ak_hm"], fp["av_hm"], (mask, btiles), seq)
    kvb_p = _mm(fp["ckv_b"], w["wkv"], "heads", 512)
    ob_p = _flash("mla", fp["qcat"], kvb_p, kvb_p, (fp["krp"],), seq)
    y_p = _back(xp, fp, oa_p, ob_p, w, norm_ffn_g[0], final_norm_g, 256, 512)

    m_s = nb * n_new
    xs = x_sample.reshape(m_s, d)
    pos_s = jnp.tile(past + jnp.arange(n_new, dtype=jnp.int32), nb)
    fs = _front(xs, pos_s, w, norm_mix_g[0], q_lora_g[0], kv_lora_g[0], m_s)
    ixk_s = fs["zmisc"][:, MISC_IXK:MISC_IXK + IDX_DIM]

    def with_cache(cache, new, width):
        a = jnp.concatenate([cache.reshape(nb, past, width).astype(BF16),
                             new.reshape(nb, n_new, width).astype(BF16)], axis=1)
        return jnp.pad(a, ((0, 0), (0, sp - s_valid), (0, 0)))

    kc = jnp.transpose(cache_a_k[0].astype(BF16), (0, 2, 1, 3))
    vc = jnp.transpose(cache_a_v[0].astype(BF16), (0, 2, 1, 3))
    ixk_all = jnp.swapaxes(with_cache(cache_a_idx_k[0], ixk_s, IDX_DIM), 1, 2)
    ckv_all = with_cache(cache_b_ckv[0], fs["ckv"], KV_LORA)
    krope_pad = jnp.pad(cache_b_krope[0], ((0, 0), (0, 0), (0, LANES - ROPE_DIM)))
    krp_all = with_cache(krope_pad, fs["krp"], LANES)
    kvb_s = _mm(ckv_all.reshape(nb * sp, KV_LORA), w["wkv"], "bf16", sp).reshape(nb, sp, 2 * hd)
    qpos = past + jnp.arange(n_new, dtype=jnp.int32)
    kpos = jnp.arange(sp, dtype=jnp.int32)
    bias_s = jnp.transpose(_rel_bias(rel_bias_table, kpos[None, :] - qpos[:, None]), (2, 0, 1))
    oa_s, ob_s = _sample_attn(fs["zmisc"], ixk_all, fs["aq_hm"], kc, vc, fs["ak_hm"], fs["av_hm"],
                              bias_s.astype(F32) * LOG2E,
                              fs["qcat"], kvb_s, krp_all, n_new, s_valid, past, topk_s)
    y_s = _back(xs, fs, oa_s, ob_s, w, norm_ffn_g[0], final_norm_g, m_s, m_s)

    st = lambda a, b, *tail: a.reshape((1, b, -1) + tail)
    return (y_p.reshape(1, seq, d), y_s.reshape(nb, n_new, d),
            st(fp["ak"], 1, N_HEADS, HEAD_DIM), st(fp["av"], 1, N_HEADS, HEAD_DIM),
            st(ixk_p, 1, IDX_DIM), st(fp["ckv"], 1, KV_LORA), st(fp["kr"], 1, ROPE_DIM),
            st(fs["ak"], nb, N_HEADS, HEAD_DIM), st(fs["av"], nb, N_HEADS, HEAD_DIM),
            st(ixk_s, nb, IDX_DIM), st(fs["ckv"], nb, KV_LORA), st(fs["kr"], nb, ROPE_DIM))
```

```python
import functools
import math

import numpy as np
import jax
import jax.numpy as jnp
from jax import lax
from jax.experimental import pallas as pl
from jax.experimental.pallas import tpu as pltpu

F32 = jnp.float32
BF16 = jnp.bfloat16
I32 = jnp.int32

CHUNK = 64
CHUNK_SHIFT = 6
N_HEADS = 16
HEAD_DIM = 128
IDX_HEADS = 16
IDX_DIM = 64
TOPK_MAX = 256
REL_BUCKETS = 32
REL_MAX_DIST = 128
QK_NOPE_DIM = 128
ROPE_DIM = 64
Q_LORA = 512
KV_LORA = 256
ROPE_THETA = 10000.0
MLA_SCALE = (QK_NOPE_DIM + ROPE_DIM) ** -0.5
A_SCALE = HEAD_DIM ** -0.5
EPS = 1e-6
NEG_INF = -1e30
LOG2E = math.log2(math.e)

LANES = 128
Q_CAT = 2 * LANES
V7X_VMEM_LIMIT = 56 * 1024 * 1024

MISC_IXQ = 0
MISC_CQ = 1024
MISC_CKV = 1536
MISC_KR = 1792
MISC_IXK = 1856
MISC_KRSW = 1920
MISC_IXW = 1984
MISC_COLS = 2048


def _sortable_key_const(v):
    bits = int(np.float32(v).view(np.int32))
    return bits ^ ((bits >> 31) & 0x7FFFFFFF)


KEY_MASKED = _sortable_key_const(NEG_INF)
KEY_HALF = _sortable_key_const(np.float32(0.5) * np.float32(NEG_INF))
INT_MIN = -(2 ** 31)


def _params(sem):
    return pltpu.CompilerParams(dimension_semantics=sem, vmem_limit_bytes=V7X_VMEM_LIMIT)


def _rms(x, g):
    ms = jnp.mean(x * x, axis=-1, keepdims=True)
    return (x * lax.rsqrt(ms + EPS)) * g


def _sort_key(x):
    bits = lax.bitcast_convert_type(x, I32)
    return bits ^ ((bits >> 31) & 0x7FFFFFFF)


def _lane_tile(x, n):
    return x if n == 1 else jnp.concatenate([x] * n, axis=1)


def _norm_cast_kernel(x_ref, g_ref, o_ref):
    o_ref[...] = _rms(x_ref[...], g_ref[...]).astype(o_ref.dtype)


def _norm_cast(x, g, tm):
    m, d = x.shape
    return pl.pallas_call(
        _norm_cast_kernel,
        grid=(m // tm,),
        in_specs=[pl.BlockSpec((tm, d), lambda i: (i, 0)),
                  pl.BlockSpec((1, d), lambda i: (0, 0))],
        out_specs=pl.BlockSpec((tm, d), lambda i: (i, 0)),
        out_shape=jax.ShapeDtypeStruct((m, d), BF16),
        compiler_params=_params(("parallel",)),
        name="norm_cast",
    )(x, g.reshape(1, d))


def _mm_kernel(x_ref, w_ref, *o_refs, mode, head_scale):
    acc = jnp.dot(x_ref[...], w_ref[...], preferred_element_type=F32)
    if mode == "f32":
        o_refs[0][...] = acc
    elif mode == "sigmoid":
        o_refs[0][...] = jax.nn.sigmoid(acc)
    elif mode == "bf16":
        o_refs[0][...] = acc.astype(BF16)
    else:
        if mode == "f32+heads":
            o_refs[0][...] = acc
        hm = o_refs[-1]
        for h in range(hm.shape[0]):
            blk = acc[:, h * LANES:(h + 1) * LANES]
            hm[h] = (blk if head_scale == 1.0 else blk * head_scale).astype(BF16)


def _mm(x, w, mode, tm, head_scale=1.0):
    m, k = x.shape
    n = w.shape[1]
    flat = pl.BlockSpec((tm, n), lambda i: (i, 0))
    heads = pl.BlockSpec((n // LANES, tm, LANES), lambda i: (0, i, 0))
    hm_shape = jax.ShapeDtypeStruct((n // LANES, m, LANES), BF16)
    if mode in ("f32", "sigmoid"):
        out_specs, out_shape = flat, jax.ShapeDtypeStruct((m, n), F32)
    elif mode == "bf16":
        out_specs, out_shape = flat, jax.ShapeDtypeStruct((m, n), BF16)
    elif mode == "heads":
        out_specs, out_shape = heads, hm_shape
    else:
        out_specs = [flat, heads]
        out_shape = [jax.ShapeDtypeStruct((m, n), F32), hm_shape]
    return pl.pallas_call(
        functools.partial(_mm_kernel, mode=mode, head_scale=head_scale),
        grid=(m // tm,),
        in_specs=[pl.BlockSpec((tm, k), lambda i: (i, 0)),
                  pl.BlockSpec((k, n), lambda i: (0, 0))],
        out_specs=out_specs,
        out_shape=out_shape,
        compiler_params=_params(("parallel",)),
        name="mm_" + mode.replace("+", "_"),
    )(x, w)


def _mla_prep_kernel(cq_ref, ckv_ref, a_ref, b_ref, cos_ref, sin_ref, gq_ref, gkv_ref,
                     wq_ref, wsw_ref, qcat_ref, ckv_o, ckvb_o, kr_o, krp_o):
    cq = _rms(cq_ref[...], gq_ref[...]).astype(BF16)
    ckv = _rms(ckv_ref[...], gkv_ref[...])
    ckv_o[...] = ckv
    ckvb_o[...] = ckv.astype(BF16)
    cos = cos_ref[...]
    sin = sin_ref[...]
    r = a_ref[...] * cos + b_ref[...] * sin
    kr_o[...] = r[:, :ROPE_DIM]
    krp_o[...] = r.astype(BF16)
    for h in range(N_HEADS):
        qh = jnp.dot(cq, wq_ref[:, h * Q_CAT:(h + 1) * Q_CAT], preferred_element_type=F32)
        qs = jnp.dot(cq, wsw_ref[:, h * LANES:(h + 1) * LANES], preferred_element_type=F32)
        rp = qh[:, LANES:] * cos + qs * sin
        qcat = jnp.concatenate([qh[:, :LANES], rp], axis=1) * (MLA_SCALE * LOG2E)
        qcat_ref[h] = qcat.astype(BF16)


def _mla_prep(zmisc, cos_t, sin_t, gq, gkv, wq_cat, wq_sw, tm):
    m = zmisc.shape[0]
    row = lambda w, j: pl.BlockSpec((tm, w), lambda i, j=j: (i, j))
    const = lambda a: pl.BlockSpec(a.shape, lambda i: (0,) * a.ndim)
    gq2, gkv2 = gq.reshape(1, Q_LORA), gkv.reshape(1, KV_LORA)
    return pl.pallas_call(
        _mla_prep_kernel,
        grid=(m // tm,),
        in_specs=[row(Q_LORA, MISC_CQ // Q_LORA), row(KV_LORA, MISC_CKV // KV_LORA),
                  row(LANES, MISC_KR // LANES), row(LANES, MISC_KRSW // LANES),
                  row(LANES, 0), row(LANES, 0), const(gq2), const(gkv2),
                  const(wq_cat), const(wq_sw)],
        out_specs=[pl.BlockSpec((N_HEADS, tm, Q_CAT), lambda i: (0, i, 0)),
                   row(KV_LORA, 0), row(KV_LORA, 0), row(ROPE_DIM, 0), row(LANES, 0)],
        out_shape=[jax.ShapeDtypeStruct((N_HEADS, m, Q_CAT), BF16),
                   jax.ShapeDtypeStruct((m, KV_LORA), F32),
                   jax.ShapeDtypeStruct((m, KV_LORA), BF16),
                   jax.ShapeDtypeStruct((m, ROPE_DIM), F32),
                   jax.ShapeDtypeStruct((m, LANES), BF16)],
        compiler_params=_params(("parallel",)),
        name="mla_prep",
    )(zmisc, zmisc, zmisc, zmisc, cos_t, sin_t, gq2, gkv2, wq_cat, wq_sw)


SCORE_TILE = 256


def _indexer_kernel(q_ref, wb_ref, kT_ref, mask_ref, key_sc, w_sc, q_sc, *, tq, ts, cw, topk):
    i = pl.program_id(0)
    s_total = kT_ref.shape[2]
    t0 = i * tq
    front = t0 + tq
    nblk = (front + ts - 1) // ts
    nch = nblk * (ts // cw)

    w = wb_ref[...]
    off = MISC_IXW - MISC_KRSW
    for h in range(IDX_HEADS):
        w_sc[h] = jnp.broadcast_to(w[:, off + h:off + h + 1] * (IDX_HEADS ** -0.5), (tq, LANES))
    qb = q_ref[...].astype(BF16)
    npair = IDX_HEADS // 2
    for j in range(npair):
        q_sc[j * tq:(j + 1) * tq, :] = qb[:, j * LANES:(j + 1) * LANES]

    def score_block(kb, rmax):
        c0 = pl.multiple_of(kb * ts, ts)
        for c in range(ts // SCORE_TILE):
            cc = pl.multiple_of(c0 + c * SCORE_TILE, SCORE_TILE)
            w_both = jnp.concatenate([kT_ref[0, :, pl.ds(cc, SCORE_TILE)],
                                      kT_ref[1, :, pl.ds(cc, SCORE_TILE)]], axis=1)
            d_both = jnp.dot(q_sc[...], w_both, preferred_element_type=F32)
            acc = jnp.zeros((tq, SCORE_TILE), F32)
            for h in range(IDX_HEADS):
                d = d_both[(h // 2) * tq:(h // 2 + 1) * tq, (h % 2) * SCORE_TILE:(h % 2 + 1) * SCORE_TILE]
                acc = acc + jnp.maximum(d, 0.0) * _lane_tile(w_sc[h], SCORE_TILE // LANES)
            rows = t0 + lax.broadcasted_iota(I32, (tq, SCORE_TILE), 0)
            cols = cc + lax.broadcasted_iota(I32, (tq, SCORE_TILE), 1)
            adm = (cols >> CHUNK_SHIFT) <= (rows >> CHUNK_SHIFT)
            sc = jnp.where(adm, acc, NEG_INF)
            bits = lax.bitcast_convert_type(sc, I32)
            key_sc[:, pl.ds(cc, SCORE_TILE)] = bits ^ ((bits >> 31) & 0x7FFFFFFF)
            for j in range(SCORE_TILE // LANES):
                rmax = jnp.maximum(rmax, sc[:, j * LANES:(j + 1) * LANES])
        return rmax

    rmax = lax.fori_loop(0, nblk, score_block, jnp.full((tq, LANES), NEG_INF, F32))

    kf = float(topk)

    def count_ge(cand):
        def body(c, cnt):
            c0 = pl.multiple_of(c * cw, cw)
            k = key_sc[:, pl.ds(c0, cw)]
            for j in range(cw // LANES):
                cnt = cnt + jnp.where(k[:, j * LANES:(j + 1) * LANES] >= cand, 1.0, 0.0)
            return cnt
        cnt = lax.fori_loop(0, nch, body, jnp.zeros((tq, LANES), F32))
        return jnp.broadcast_to(jnp.sum(cnt, axis=1, keepdims=True), (tq, LANES))

    rows1 = t0 + lax.broadcasted_iota(I32, (tq, LANES), 0)
    few = (((rows1 >> CHUNK_SHIFT) + 1) << CHUNK_SHIFT) < topk
    row_max = jnp.broadcast_to(jnp.max(rmax, axis=1, keepdims=True), (tq, LANES))
    lo = jnp.full((tq, LANES), KEY_MASKED, I32)
    hi = _sort_key(row_max) + 1
    cnt_lo = jnp.broadcast_to((nch * cw).astype(F32), (tq, LANES))
    cnt_hi = jnp.zeros((tq, LANES), F32)
    log_k = math.log(topk)

    def key_value(k):
        return lax.bitcast_convert_type(k ^ ((k >> 31) & 0x7FFFFFFF), F32)

    def done_of(lo, hi, cnt_lo):
        return few | (cnt_lo == kf) | (hi - 1 <= lo)

    def search_cond(st):
        return st[-1] > 0.5

    scale = jnp.maximum(jnp.abs(row_max), 1e-30)
    ones_v = jnp.ones((tq, LANES), F32)

    def search_step(st):
        lo, hi, cnt_lo, cnt_hi, w_lo, w_hi, last, step, _ = st
        done = done_of(lo, hi, cnt_lo)
        mid = (lo >> 1) + (hi >> 1) + (lo & hi & 1)
        lo_v, hi_v = key_value(lo), key_value(hi)
        f_lo = (jnp.log(cnt_lo) - log_k) * w_lo
        f_hi = (jnp.log(jnp.maximum(cnt_hi, 0.5)) - log_k) * w_hi
        guess = _sort_key(lo_v + f_lo / (f_lo - f_hi) * (hi_v - lo_v))
        grow = jnp.exp2(jnp.broadcast_to(jnp.minimum(step - 1, 100).astype(F32), (tq, LANES)))
        guess = jnp.where(lo <= KEY_HALF, _sort_key(-scale * grow), guess)
        first = jnp.zeros((tq, LANES), I32)
        cand = jnp.where(step == 0, first, jnp.where(step % 4 == 3, mid, guess))
        cand = jnp.minimum(jnp.maximum(cand, lo + 1), hi - 1)
        cnt = count_ge(cand)
        up = (cnt >= kf) & jnp.logical_not(done)
        down = (cnt < kf) & jnp.logical_not(done)
        lo = jnp.where(up, cand, lo)
        cnt_lo = jnp.where(up, cnt, cnt_lo)
        hi = jnp.where(down, cand, hi)
        cnt_hi = jnp.where(down, cnt, cnt_hi)
        w_hi = jnp.where(up, jnp.where(last > 0.5, 0.5 * w_hi, ones_v), jnp.where(down, ones_v, w_hi))
        w_lo = jnp.where(down, jnp.where(last < -0.5, 0.5 * w_lo, ones_v), jnp.where(up, ones_v, w_lo))
        last = jnp.where(up, 1.0, jnp.where(down, -1.0, last))
        active = jnp.max(jnp.where(done_of(lo, hi, cnt_lo), 0.0, 1.0))
        return lo, hi, cnt_lo, cnt_hi, w_lo, w_hi, last, step + 1, active

    active0 = jnp.max(jnp.where(done_of(lo, hi, cnt_lo), 0.0, 1.0))
    lo, hi, cnt_lo, cnt_hi = lax.while_loop(
        search_cond, search_step,
        (lo, hi, cnt_lo, cnt_hi, ones_v, ones_v, jnp.zeros((tq, LANES), F32), jnp.int32(0), active0))[:4]
    ans = jnp.where(few, INT_MIN, lo)
    thr = _lane_tile(jnp.maximum(ans, KEY_HALF + 1), cw // LANES)
    tied = jnp.logical_not(few) & (cnt_lo > kf) & (lo > KEY_HALF)
    any_tied = jnp.max(jnp.where(tied, 1.0, 0.0))

    def write_mask(c, carry):
        c0 = pl.multiple_of(c * cw, cw)
        sel = key_sc[:, pl.ds(c0, cw)] >= thr
        mask_ref[:, pl.ds(c0, cw)] = jnp.where(sel, 0.0, NEG_INF).astype(BF16)
        return carry

    @pl.when(any_tied < 0.5)
    def _plain():
        lax.fori_loop(0, nch, write_mask, 0)

    @pl.when(any_tied >= 0.5)
    def _with_ties():
        need = kf - count_ge(lo + 1)
        lo_t = _lane_tile(lo, cw // LANES)
        tied_t = _lane_tile(jnp.where(tied, 1.0, 0.0), cw // LANES) > 0.5
        need_t = _lane_tile(need, cw // LANES)
        jj = lax.broadcasted_iota(I32, (cw, cw), 0)
        ii = lax.broadcasted_iota(I32, (cw, cw), 1)
        upper = jnp.where(jj <= ii, 1.0, 0.0).astype(BF16)

        def write_tied(c, seen):
            c0 = pl.multiple_of(c * cw, cw)
            k = key_sc[:, pl.ds(c0, cw)]
            eq = jnp.where(k == lo_t, 1.0, 0.0)
            incl = jnp.dot(eq.astype(BF16), upper, preferred_element_type=F32)
            rank = _lane_tile(seen, cw // LANES) + incl - eq
            take = (k > lo_t) | ((eq > 0.5) & (rank < need_t))
            sel = (tied_t & take) | (jnp.logical_not(tied_t) & (k >= thr))
            mask_ref[:, pl.ds(c0, cw)] = jnp.where(sel, 0.0, NEG_INF).astype(BF16)
            return seen + jnp.broadcast_to(incl[:, cw - 1:cw], (tq, LANES))

        lax.fori_loop(0, nch, write_tied, jnp.zeros((tq, LANES), F32))

    def zero_mask(c, carry):
        c0 = pl.multiple_of(c * cw, cw)
        mask_ref[:, pl.ds(c0, cw)] = jnp.full((tq, cw), NEG_INF, BF16)
        return carry

    lax.fori_loop(nch, s_total // cw, zero_mask, 0)


def _indexer(zmisc, ixk_t, topk, tq=128, ts=1024, cw=512):
    s = ixk_t.shape[1]
    assert ts % cw == 0 and s % ts == 0 and ts % tq == 0
    zero = jnp.zeros_like(ixk_t)
    ixk_t = jnp.stack([jnp.concatenate([ixk_t, zero], axis=0), jnp.concatenate([zero, ixk_t], axis=0)])
    return pl.pallas_call(
        functools.partial(_indexer_kernel, tq=tq, ts=ts, cw=cw, topk=topk),
        grid=(s // tq,),
        in_specs=[pl.BlockSpec((tq, IDX_HEADS * IDX_DIM), lambda i: (i, 0)),
                  pl.BlockSpec((tq, LANES), lambda i: (i, MISC_KRSW // LANES)),
                  pl.BlockSpec((2, 2 * IDX_DIM, s), lambda i: (0, 0, 0))],
        out_specs=pl.BlockSpec((tq, s), lambda i: (i, 0)),
        out_shape=jax.ShapeDtypeStruct((s, s), BF16),
        scratch_shapes=[pltpu.VMEM((tq, s), I32),
                        pltpu.VMEM((IDX_HEADS, tq, LANES), F32),
                        pltpu.VMEM((IDX_HEADS // 2 * tq, 2 * IDX_DIM), BF16)],
        compiler_params=_params(("parallel",)),
        name="indexer",
    )(zmisc, zmisc, ixk_t)


_NT = (((1,), (1,)), ((), ()))
FLASH_AHEAD = 2
FLASH_GROUP = 16


def _flash_kernel(qi_ref, ki_ref, *refs, kind, tb):
    if kind == "dsa":
        q_ref, k_ref, v_ref, mask_ref, b_ref, o_ref, m_sc, l_sc, acc_sc, mb_sc = refs
    else:
        q_ref, k_ref, kr_ref, v_ref, o_ref, m_sc, l_sc, acc_sc, mb_sc = refs
    p = pl.program_id(0)
    qi = qi_ref[p]
    ki = ki_ref[p]
    nsub = tb // LANES

    @pl.when(ki == 0)
    def _init():
        m_sc[...] = jnp.full(m_sc.shape, NEG_INF, F32)
        l_sc[...] = jnp.zeros(l_sc.shape, F32)
        acc_sc[...] = jnp.zeros(acc_sc.shape, F32)

    ones = jnp.ones((tb, LANES), BF16)

    def update(h, s):
        m_prev = m_sc[h]
        m_new = jnp.maximum(m_prev, jnp.max(s, axis=1, keepdims=True))
        alpha = jnp.exp2(m_prev - m_new)
        pr = jnp.exp2(s - _lane_tile(m_new, nsub)).astype(BF16)
        pv = jnp.dot(pr, jnp.concatenate([v_ref[h], ones], axis=1), preferred_element_type=F32)
        acc_sc[h] = alpha * acc_sc[h] + pv[:, :LANES]
        l_sc[h] = alpha * l_sc[h] + pv[:, LANES:]
        m_sc[h] = m_new

    def run_heads(logits):
        def group(g, carry):
            base = g * FLASH_GROUP
            ahead = [logits(base + j) for j in range(min(FLASH_AHEAD, FLASH_GROUP))]
            for j in range(FLASH_GROUP):
                s = ahead.pop(0)
                if j + FLASH_AHEAD < FLASH_GROUP:
                    ahead.append(logits(base + j + FLASH_AHEAD))
                update(base + j, s)
            return carry

        lax.fori_loop(0, N_HEADS // FLASH_GROUP, group, 0)

    if kind == "dsa":
        mb_sc[...] = mask_ref[...].astype(F32)

        def logits_far(h):
            return lax.dot_general(q_ref[h], k_ref[h], _NT, preferred_element_type=F32) + mb_sc[...]

        on_diag = ki == qi
        i_main = jnp.where(on_diag, 0, 2)
        i_sub = jnp.where(on_diag, 1, 2)
        i_corner = jnp.where(on_diag, 2, 1)

        def logits_near(h):
            s = logits_far(h)
            rows = []
            for a in range(nsub):
                blocks = []
                for b in range(nsub):
                    blk = s[a * LANES:(a + 1) * LANES, b * LANES:(b + 1) * LANES]
                    if b == a:
                        blk = blk + b_ref[i_main, h]
                    elif b == a - 1:
                        blk = blk + b_ref[i_sub, h]
                    elif a == 0 and b == nsub - 1:
                        blk = blk + b_ref[i_corner, h]
                    blocks.append(blk)
                rows.append(jnp.concatenate(blocks, axis=1))
            return jnp.concatenate(rows, axis=0)

        @pl.when(ki < qi - 1)
        def _far():
            run_heads(logits_far)

        @pl.when(ki >= qi - 1)
        def _near():
            run_heads(logits_near)
    else:
        @pl.when(p == 0)
        def _masks():
            rows = lax.broadcasted_iota(I32, (tb, tb), 0)
            cols = lax.broadcasted_iota(I32, (tb, tb), 1)
            vis = (cols >> CHUNK_SHIFT) <= (rows >> CHUNK_SHIFT)
            mb_sc[...] = jnp.where(vis, 0.0, NEG_INF)

        def qk(h):
            k = jnp.concatenate([k_ref[h], kr_ref[...]], axis=1)
            return lax.dot_general(q_ref[h], k, _NT, preferred_element_type=F32)

        @pl.when(ki < qi)
        def _off():
            run_heads(qk)

        @pl.when(ki == qi)
        def _diag():
            run_heads(lambda h: qk(h) + mb_sc[...])

    @pl.when(ki == qi)
    def _finish():
        for h in range(N_HEADS):
            o_ref[:, h * LANES:(h + 1) * LANES] = acc_sc[h] / l_sc[h]


def _flash(kind, q, k, v, extra, seq, tb=512):
    nq = seq // tb
    pairs = [(a, b) for a in range(nq) for b in range(a + 1)]
    qi = jnp.asarray([a for a, _ in pairs], I32)
    ki = jnp.asarray([b for _, b in pairs], I32)
    dq = q.shape[2]
    qspec = pl.BlockSpec((N_HEADS, tb, dq), lambda p, qi, ki: (0, qi[p], 0))
    scratch = [pltpu.VMEM((N_HEADS, tb, LANES), F32), pltpu.VMEM((N_HEADS, tb, LANES), F32),
               pltpu.VMEM((N_HEADS, tb, LANES), F32)]
    if kind == "dsa":
        mask, btiles = extra
        in_specs = [qspec,
                    pl.BlockSpec((N_HEADS, tb, LANES), lambda p, qi, ki: (0, ki[p], 0)),
                    pl.BlockSpec((N_HEADS, tb, LANES), lambda p, qi, ki: (0, ki[p], 0)),
                    pl.BlockSpec((tb, tb), lambda p, qi, ki: (qi[p], ki[p])),
                    pl.BlockSpec(btiles.shape, lambda p, qi, ki: (0, 0, 0, 0))]
        args = (q, k, v, mask, btiles)
        scratch = scratch + [pltpu.VMEM((tb, tb), F32)]
    else:
        (krp,) = extra
        in_specs = [qspec,
                    pl.BlockSpec((N_HEADS, tb, LANES), lambda p, qi, ki: (0, ki[p], 0)),
                    pl.BlockSpec((tb, LANES), lambda p, qi, ki: (ki[p], 0)),
                    pl.BlockSpec((N_HEADS, tb, LANES), lambda p, qi, ki: (1, ki[p], 0))]
        args = (q, k, krp, v)
        scratch = scratch + [pltpu.VMEM((tb, tb), F32)]
    return pl.pallas_call(
        functools.partial(_flash_kernel, kind=kind, tb=tb),
        grid_spec=pltpu.PrefetchScalarGridSpec(
            num_scalar_prefetch=2,
            grid=(len(pairs),),
            in_specs=in_specs,
            out_specs=pl.BlockSpec((tb, N_HEADS * LANES), lambda p, qi, ki: (qi[p], 0)),
            scratch_shapes=scratch),
        out_shape=jax.ShapeDtypeStruct((seq, N_HEADS * LANES), F32),
        compiler_params=_params(("arbitrary",)),
        name="flash_" + kind,
    )(qi, ki, *args)


def _sample_kernel(ixq_ref, wb_ref, ikT_ref, aq_ref, kc_ref, vc_ref, kn_ref, vn_ref, bias_ref, qc_ref,
                   kvb_ref, krp_ref, oa_ref, ob_ref, *, n_new, s_valid, past, topk):
    sp = ikT_ref.shape[2]
    qpos = past + lax.broadcasted_iota(I32, (n_new, sp), 0)
    kpos = lax.broadcasted_iota(I32, (n_new, sp), 1)
    adm = (kpos < s_valid) & ((kpos >> CHUNK_SHIFT) <= (qpos >> CHUNK_SHIFT))

    qb = ixq_ref[...].astype(BF16)
    w = wb_ref[...]
    kT = ikT_ref[0]
    off = MISC_IXW - MISC_KRSW
    score = jnp.zeros((n_new, sp), F32)
    for h in range(IDX_HEADS):
        d = jnp.dot(qb[:, h * IDX_DIM:(h + 1) * IDX_DIM], kT, preferred_element_type=F32)
        score = score + jnp.maximum(d, 0.0) * (w[:, off + h:off + h + 1] * (IDX_HEADS ** -0.5))
    key = jnp.where(adm, _sort_key(score), KEY_MASKED)

    kf = float(topk)

    def count_ge(cand):
        return jnp.sum(jnp.where(key >= cand, 1.0, 0.0), axis=1, keepdims=True)

    ans = jnp.where(count_ge(jnp.zeros((n_new, 1), I32)) >= kf, 0, INT_MIN).astype(I32)

    def bit_step(b, ans):
        cand = ans + jnp.left_shift(jnp.int32(1), 30 - b)
        return jnp.where(count_ge(cand) >= kf, cand, ans)

    ans = lax.fori_loop(0, 31, bit_step, ans)
    need = kf - count_ge(ans + 1)
    eq = jnp.where(key == ans, 1.0, 0.0)
    jj = lax.broadcasted_iota(I32, (sp, sp), 0)
    ii = lax.broadcasted_iota(I32, (sp, sp), 1)
    before = jnp.where(jj < ii, 1.0, 0.0).astype(BF16)
    rank = jnp.dot(eq.astype(BF16), before, preferred_element_type=F32)
    sel = ((key > ans) | ((eq > 0.5) & (rank < need))) & (key > KEY_HALF)

    def attend(s, vmat):
        m = jnp.max(s, axis=1, keepdims=True)
        pr = jnp.exp2(s - m)
        l = jnp.sum(pr, axis=1, keepdims=True)
        return jnp.dot(pr.astype(BF16), vmat, preferred_element_type=F32) / l

    row_pad = jnp.zeros((sp - past - n_new, LANES), BF16)
    for h in range(N_HEADS):
        hs = slice(h * LANES, (h + 1) * LANES)
        kn = jnp.concatenate([kn_ref[h], row_pad], axis=0)
        vn = jnp.concatenate([vn_ref[h], row_pad], axis=0)
        q = aq_ref[h]
        s = jnp.concatenate([lax.dot_general(q, kc_ref[0, h], _NT, preferred_element_type=F32),
                             lax.dot_general(q, kn, _NT, preferred_element_type=F32)], axis=1)
        s = jnp.where(sel, s + bias_ref[h], NEG_INF)
        m = jnp.max(s, axis=1, keepdims=True)
        pr = jnp.exp2(s - m)
        l = jnp.sum(pr, axis=1, keepdims=True)
        pr = pr.astype(BF16)
        pv = (jnp.dot(pr[:, :past], vc_ref[0, h], preferred_element_type=F32)
              + jnp.dot(pr[:, past:], vn, preferred_element_type=F32))
        oa_ref[:, hs] = pv / l

    krp = krp_ref[0]
    hd = N_HEADS * LANES
    for h in range(N_HEADS):
        hs = slice(h * LANES, (h + 1) * LANES)
        k = jnp.concatenate([kvb_ref[0, :, hs], krp], axis=1)
        s = lax.dot_general(qc_ref[h], k, _NT, preferred_element_type=F32)
        s = jnp.where(adm, s, NEG_INF)
        ob_ref[:, hs] = attend(s, kvb_ref[0, :, hd + h * LANES:hd + (h + 1) * LANES])


def _sample_attn(zmisc, ixk_t, aq_hm, kc, vc, kn_hm, vn_hm, bias, qcat, kvb, krp, n_new, s_valid, past,
                 topk):
    nb = kc.shape[0]
    hd = N_HEADS * LANES
    assert past % LANES == 0 and n_new <= LANES and ixk_t.shape[2] == past + LANES
    new_rows = pl.BlockSpec((N_HEADS, n_new, LANES), lambda b: (0, b, 0))
    per_stream = lambda a: pl.BlockSpec((1,) + a.shape[1:], lambda b: (b,) + (0,) * (a.ndim - 1))
    return pl.pallas_call(
        functools.partial(_sample_kernel, n_new=n_new, s_valid=s_valid, past=past, topk=topk),
        grid=(nb,),
        in_specs=[pl.BlockSpec((n_new, IDX_HEADS * IDX_DIM), lambda b: (b, 0)),
                  pl.BlockSpec((n_new, LANES), lambda b: (b, MISC_KRSW // LANES)),
                  per_stream(ixk_t),
                  new_rows, per_stream(kc), per_stream(vc), new_rows, new_rows,
                  pl.BlockSpec(bias.shape, lambda b: (0, 0, 0)),
                  pl.BlockSpec((N_HEADS, n_new, Q_CAT), lambda b: (0, b, 0)),
                  per_stream(kvb), per_stream(krp)],
        out_specs=[pl.BlockSpec((n_new, hd), lambda b: (b, 0)),
                   pl.BlockSpec((n_new, hd), lambda b: (b, 0))],
        out_shape=[jax.ShapeDtypeStruct((nb * n_new, hd), F32),
                   jax.ShapeDtypeStruct((nb * n_new, hd), F32)],
        compiler_params=_params(("parallel",)),
        name="sample_attn",
    )(zmisc, zmisc, ixk_t, aq_hm, kc, vc, kn_hm, vn_hm, bias, qcat, kvb, krp)


def _outproj_kernel(x_ref, ga_ref, gb_ref, oa_ref, ob_ref, w_ref, o_ref):
    mix = ga_ref[...] * oa_ref[...] + gb_ref[...] * ob_ref[...]
    o_ref[...] = x_ref[...] + jnp.dot(mix.astype(BF16), w_ref[...], preferred_element_type=F32)


def _outproj(x, ga, gb, oa, ob, w, tm):
    m, d = x.shape
    row = pl.BlockSpec((tm, d), lambda i: (i, 0))
    return pl.pallas_call(
        _outproj_kernel,
        grid=(m // tm,),
        in_specs=[row, row, row, row, row, pl.BlockSpec(w.shape, lambda i: (0, 0))],
        out_specs=row,
        out_shape=jax.ShapeDtypeStruct((m, d), F32),
        compiler_params=_params(("parallel",)),
        name="outproj",
    )(x, ga, gb, oa, ob, w)


def _ffn_kernel(x_ref, g_ref, wu_ref, wd_ref, gf_ref, o_ref, h_sc, acc_sc):
    f = pl.program_id(1)

    @pl.when(f == 0)
    def _start():
        h_sc[...] = _rms(x_ref[...], g_ref[...]).astype(BF16)
        acc_sc[...] = jnp.zeros(acc_sc.shape, F32)

    u = jnp.dot(h_sc[...], wu_ref[...], preferred_element_type=F32)
    u = jnp.square(jnp.maximum(u, 0.0)).astype(BF16)
    acc_sc[...] += jnp.dot(u, wd_ref[...], preferred_element_type=F32)

    @pl.when(f == pl.num_programs(1) - 1)
    def _end():
        o_ref[...] = _rms(x_ref[...] + acc_sc[...], gf_ref[...])


def _ffn(x, g, wu, wd, gf, tm, tf=1024):
    m, d = x.shape
    dff = wu.shape[1]
    return pl.pallas_call(
        _ffn_kernel,
        grid=(m // tm, dff // tf),
        in_specs=[pl.BlockSpec((tm, d), lambda i, f: (i, 0)),
                  pl.BlockSpec((1, d), lambda i, f: (0, 0)),
                  pl.BlockSpec((d, tf), lambda i, f: (0, f)),
                  pl.BlockSpec((tf, d), lambda i, f: (f, 0)),
                  pl.BlockSpec((1, d), lambda i, f: (0, 0))],
        out_specs=pl.BlockSpec((tm, d), lambda i, f: (i, 0)),
        out_shape=jax.ShapeDtypeStruct((m, d), F32),
        scratch_shapes=[pltpu.VMEM((tm, d), BF16), pltpu.VMEM((tm, d), F32)],
        compiler_params=_params(("parallel", "arbitrary")),
        name="ffn",
    )(x, g.reshape(1, d), wu, wd, gf.reshape(1, d))


def _t5_bucket(rel):
    nb = REL_BUCKETS // 2
    ret = (rel > 0).astype(jnp.int32) * nb
    n = jnp.abs(rel)
    max_exact = nb // 2
    large = max_exact + (jnp.log(jnp.maximum(n, 1).astype(jnp.float32) / max_exact)
                         / math.log(REL_MAX_DIST / max_exact) * (nb - max_exact)).astype(jnp.int32)
    large = jnp.minimum(large, nb - 1)
    return ret + jnp.where(n < max_exact, n, large)


def _rel_bias(table, rel):
    onehot = jax.nn.one_hot(_t5_bucket(rel), REL_BUCKETS, dtype=F32)
    return jnp.einsum("...b,bh->...h", onehot, table.astype(F32), precision=lax.Precision.HIGHEST)


def _rope_tables(pos):
    half = ROPE_DIM // 2
    inv_freq = jnp.power(ROPE_THETA, -jnp.arange(half, dtype=jnp.float32) / half)
    ang = pos.astype(jnp.float32)[:, None] * inv_freq[None, :]
    cos, sin = jnp.cos(ang), jnp.sin(ang)
    pad = jnp.zeros((pos.shape[0], LANES - ROPE_DIM), F32)
    return (jnp.concatenate([cos, cos, pad], axis=1), jnp.concatenate([-sin, sin, pad], axis=1))


def _swap_halves(w):
    half = ROPE_DIM // 2
    return jnp.concatenate([w[..., half:], w[..., :half]], axis=-1)


def _prep_weights(w_in, w_uq, w_uk, w_uv, w_out, w_ff_up, w_ff_down):
    d = w_in.shape[0]
    hd = N_HEADS * HEAD_DIM
    sizes = (hd, hd, hd, IDX_HEADS * IDX_DIM, IDX_DIM, IDX_HEADS, Q_LORA, KV_LORA, ROPE_DIM, d, d)
    cols, off = [], 0
    for n in sizes:
        cols.append(w_in[:, off:off + n])
        off += n
    wq, wk, wv, wixq, wixk, wixw, wcq, wckv, wkr, wga, wgb = cols
    pad = jnp.zeros((d, MISC_COLS - MISC_IXW - IDX_HEADS), w_in.dtype)
    wmisc = jnp.concatenate([wixq, wcq, wckv, wkr, wixk, _swap_halves(wkr), wixw, pad], axis=1)
    nope, rope = w_uq[..., :QK_NOPE_DIM], w_uq[..., QK_NOPE_DIM:]
    zpad = jnp.zeros(rope.shape, w_uq.dtype)
    wq_cat = jnp.concatenate([nope, rope, zpad], axis=-1).reshape(Q_LORA, N_HEADS * Q_CAT)
    wq_sw = jnp.concatenate([_swap_halves(rope), zpad], axis=-1).reshape(Q_LORA, N_HEADS * LANES)
    wkv = jnp.concatenate([w_uk.reshape(KV_LORA, hd), w_uv.reshape(KV_LORA, hd)], axis=1)
    c = lambda a: a.astype(BF16)
    return dict(q=c(wq), k=c(wk), v=c(wv), misc=c(wmisc), ga=c(wga), gb=c(wgb), wq_cat=c(wq_cat),
                wq_sw=c(wq_sw), wkv=c(wkv), out=c(w_out), up=c(w_ff_up), down=c(w_ff_down))


def _front(x, pos, w, g_mix, g_q, g_kv, tm):
    h = _norm_cast(x, g_mix, tm)
    aq_hm = _mm(h, w["q"], "heads", tm, head_scale=A_SCALE * LOG2E)
    ak, ak_hm = _mm(h, w["k"], "f32+heads", tm)
    av, av_hm = _mm(h, w["v"], "f32+heads", tm)
    zmisc = _mm(h, w["misc"], "f32", tm)
    sga = _mm(h, w["ga"], "sigmoid", tm)
    sgb = _mm(h, w["gb"], "sigmoid", tm)
    cos_t, sin_t = _rope_tables(pos)
    qcat, ckv, ckv_b, kr, krp = _mla_prep(zmisc, cos_t, sin_t, g_q, g_kv, w["wq_cat"], w["wq_sw"], tm)
    return dict(aq_hm=aq_hm, ak=ak, ak_hm=ak_hm, av=av, av_hm=av_hm, zmisc=zmisc, sga=sga, sgb=sgb,
                qcat=qcat, ckv=ckv, ckv_b=ckv_b, kr=kr, krp=krp)


def _back(x, f, oa, ob, w, g_ffn, g_final, tm_out, tm_ffn):
    x1 = _outproj(x, f["sga"], f["sgb"], oa, ob, w["out"], tm_out)
    return _ffn(x1, g_ffn, w["up"], w["down"], g_final, tm_ffn)


def kernel(x_prompt, x_sample, cache_a_k, cache_a_v, cache_a_idx_k, cache_b_ckv, cache_b_krope,
           rel_bias_table, norm_mix_g, w_in, q_lora_g, w_uq, kv_lora_g, w_uk, w_uv, w_out,
           norm_ffn_g, w_ff_up, w_ff_down, final_norm_g):
    assert w_in.shape[0] == 1, "single-layer trunk"
    _, seq, d = x_prompt.shape
    nb, n_new, _ = x_sample.shape
    past = cache_a_k.shape[2]
    s_valid = past + n_new
    sp = -(-s_valid // LANES) * LANES
    topk_p = min(TOPK_MAX, seq // 4)
    topk_s = min(TOPK_MAX, s_valid // 4)
    hd = N_HEADS * HEAD_DIM

    w = _prep_weights(w_in[0], w_uq[0], w_uk[0], w_uv[0], w_out[0], w_ff_up[0], w_ff_down[0])

    xp = x_prompt.reshape(seq, d)
    fp = _front(xp, jnp.arange(seq, dtype=jnp.int32), w, norm_mix_g[0], q_lora_g[0], kv_lora_g[0], 512)
    ixk_p = fp["zmisc"][:, MISC_IXK:MISC_IXK + IDX_DIM]
    mask = _indexer(fp["zmisc"], ixk_p.astype(BF16).T, topk_p)
    ii = jnp.arange(LANES, dtype=jnp.int32)
    rel0 = ii[None, :] - ii[:, None]
    far = _rel_bias(rel_bias_table, jnp.full((1, 1), -REL_MAX_DIST, jnp.int32))
    btiles = jnp.stack([_rel_bias(rel_bias_table, rel0) - far,
                        _rel_bias(rel_bias_table, rel0 - LANES) - far,
                        jnp.zeros((LANES, LANES, N_HEADS), F32)])
    btiles = jnp.transpose(btiles, (0, 3, 1, 2)).astype(F32) * LOG2E
    oa_p = _flash("dsa", fp["aq_hm"], fp["ak_hm"], fp["av_hm"], (mask, btiles), seq)
    kvb_p = _mm(fp["ckv_b"], w["wkv"], "heads", 512)
    ob_p = _flash("mla", fp["qcat"], kvb_p, kvb_p, (fp["krp"],), seq)
    y_p = _back(xp, fp, oa_p, ob_p, w, norm_ffn_g[0], final_norm_g, 256, 512)

    m_s = nb * n_new
    xs = x_sample.reshape(m_s, d)
    pos_s = jnp.tile(past + jnp.arange(n_new, dtype=jnp.int32), nb)
    fs = _front(xs, pos_s, w, norm_mix_g[0], q_lora_g[0], kv_lora_g[0], m_s)
    ixk_s = fs["zmisc"][:, MISC_IXK:MISC_IXK + IDX_DIM]

    def with_cache(cache, new, width):
        a = jnp.concatenate([cache.reshape(nb, past, width).astype(BF16),
                             new.reshape(nb, n_new, width).astype(BF16)], axis=1)
        return jnp.pad(a, ((0, 0), (0, sp - s_valid), (0, 0)))

    kc = jnp.transpose(cache_a_k[0].astype(BF16), (0, 2, 1, 3))
    vc = jnp.transpose(cache_a_v[0].astype(BF16), (0, 2, 1, 3))
    ixk_all = jnp.swapaxes(with_cache(cache_a_idx_k[0], ixk_s, IDX_DIM), 1, 2)
    ckv_all = with_cache(cache_b_ckv[0], fs["ckv"], KV_LORA)
    krope_pad = jnp.pad(cache_b_krope[0], ((0, 0), (0, 0), (0, LANES - ROPE_DIM)))
    krp_all = with_cache(krope_pad, fs["krp"], LANES)
    kvb_s = _mm(ckv_all.reshape(nb * sp, KV_LORA), w["wkv"], "bf16", sp).reshape(nb, sp, 2 * hd)
    qpos = past + jnp.arange(n_new, dtype=jnp.int32)
    kpos = jnp.arange(sp, dtype=jnp.int32)
    bias_s = jnp.transpose(_rel_bias(rel_bias_table, kpos[None, :] - qpos[:, None]), (2, 0, 1))
    oa_s, ob_s = _sample_attn(fs["zmisc"], ixk_all, fs["aq_hm"], kc, vc, fs["ak_hm"], fs["av_hm"],
                              bias_s.astype(F32) * LOG2E,
                              fs["qcat"], kvb_s, krp_all, n_new, s_valid, past, topk_s)
    y_s = _back(xs, fs, oa_s, ob_s, w, norm_ffn_g[0], final_norm_g, m_s, m_s)

    st = lambda a, b, *tail: a.reshape((1, b, -1) + tail)
    return (y_p.reshape(1, seq, d), y_s.reshape(nb, n_new, d),
            st(fp["ak"], 1, N_HEADS, HEAD_DIM), st(fp["av"], 1, N_HEADS, HEAD_DIM),
            st(ixk_p, 1, IDX_DIM), st(fp["ckv"], 1, KV_LORA), st(fp["kr"], 1, ROPE_DIM),
            st(fs["ak"], nb, N_HEADS, HEAD_DIM), st(fs["av"], nb, N_HEADS, HEAD_DIM),
            st(ixk_s, nb, IDX_DIM), st(fs["ckv"], nb, KV_LORA), st(fs["kr"], nb, ROPE_DIM))
```

```python
import functools
import math

import numpy as np
import jax
import jax.numpy as jnp
from jax import lax
from jax.experimental import pallas as pl
from jax.experimental.pallas import tpu as pltpu

F32 = jnp.float32
BF16 = jnp.bfloat16
I32 = jnp.int32

CHUNK = 64
CHUNK_SHIFT = 6
N_HEADS = 16
HEAD_DIM = 128
IDX_HEADS = 16
IDX_DIM = 64
TOPK_MAX = 256
REL_BUCKETS = 32
REL_MAX_DIST = 128
QK_NOPE_DIM = 128
ROPE_DIM = 64
Q_LORA = 512
KV_LORA = 256
ROPE_THETA = 10000.0
MLA_SCALE = (QK_NOPE_DIM + ROPE_DIM) ** -0.5
A_SCALE = HEAD_DIM ** -0.5
EPS = 1e-6
NEG_INF = -1e30
LOG2E = math.log2(math.e)

LANES = 128
Q_CAT = 2 * LANES
V7X_VMEM_LIMIT = 56 * 1024 * 1024

MISC_IXQ = 0
MISC_CQ = 1024
MISC_CKV = 1536
MISC_KR = 1792
MISC_IXK = 1856
MISC_KRSW = 1920
MISC_IXW = 1984
MISC_COLS = 2048


def _sortable_key_const(v):
    bits = int(np.float32(v).view(np.int32))
    return bits ^ ((bits >> 31) & 0x7FFFFFFF)


KEY_MASKED = _sortable_key_const(NEG_INF)
KEY_HALF = _sortable_key_const(np.float32(0.5) * np.float32(NEG_INF))
INT_MIN = -(2 ** 31)


def _params(sem):
    return pltpu.CompilerParams(dimension_semantics=sem, vmem_limit_bytes=V7X_VMEM_LIMIT)


def _rms(x, g):
    ms = jnp.mean(x * x, axis=-1, keepdims=True)
    return (x * lax.rsqrt(ms + EPS)) * g


def _sort_key(x):
    bits = lax.bitcast_convert_type(x, I32)
    return bits ^ ((bits >> 31) & 0x7FFFFFFF)


def _lane_tile(x, n):
    return x if n == 1 else jnp.concatenate([x] * n, axis=1)


def _norm_cast_kernel(x_ref, g_ref, o_ref):
    o_ref[...] = _rms(x_ref[...], g_ref[...]).astype(o_ref.dtype)


def _norm_cast(x, g, tm):
    m, d = x.shape
    return pl.pallas_call(
        _norm_cast_kernel,
        grid=(m // tm,),
        in_specs=[pl.BlockSpec((tm, d), lambda i: (i, 0)),
                  pl.BlockSpec((1, d), lambda i: (0, 0))],
        out_specs=pl.BlockSpec((tm, d), lambda i: (i, 0)),
        out_shape=jax.ShapeDtypeStruct((m, d), BF16),
        compiler_params=_params(("parallel",)),
        name="norm_cast",
    )(x, g.reshape(1, d))


def _mm_kernel(x_ref, w_ref, *o_refs, mode, head_scale):
    acc = jnp.dot(x_ref[...], w_ref[...], preferred_element_type=F32)
    if mode == "f32":
        o_refs[0][...] = acc
    elif mode == "sigmoid":
        o_refs[0][...] = jax.nn.sigmoid(acc)
    elif mode == "bf16":
        o_refs[0][...] = acc.astype(BF16)
    else:
        if mode == "f32+heads":
            o_refs[0][...] = acc
        hm = o_refs[-1]
        for h in range(hm.shape[0]):
            blk = acc[:, h * LANES:(h + 1) * LANES]
            hm[h] = (blk if head_scale == 1.0 else blk * head_scale).astype(BF16)


def _mm(x, w, mode, tm, head_scale=1.0):
    m, k = x.shape
    n = w.shape[1]
    flat = pl.BlockSpec((tm, n), lambda i: (i, 0))
    heads = pl.BlockSpec((n // LANES, tm, LANES), lambda i: (0, i, 0))
    hm_shape = jax.ShapeDtypeStruct((n // LANES, m, LANES), BF16)
    if mode in ("f32", "sigmoid"):
        out_specs, out_shape = flat, jax.ShapeDtypeStruct((m, n), F32)
    elif mode == "bf16":
        out_specs, out_shape = flat, jax.ShapeDtypeStruct((m, n), BF16)
    elif mode == "heads":
        out_specs, out_shape = heads, hm_shape
    else:
        out_specs = [flat, heads]
        out_shape = [jax.ShapeDtypeStruct((m, n), F32), hm_shape]
    return pl.pallas_call(
        functools.partial(_mm_kernel, mode=mode, head_scale=head_scale),
        grid=(m // tm,),
        in_specs=[pl.BlockSpec((tm, k), lambda i: (i, 0)),
                  pl.BlockSpec((k, n), lambda i: (0, 0))],
        out_specs=out_specs,
        out_shape=out_shape,
        compiler_params=_params(("parallel",)),
        name="mm_" + mode.replace("+", "_"),
    )(x, w)


def _mla_prep_kernel(cq_ref, ckv_ref, a_ref, b_ref, cos_ref, sin_ref, gq_ref, gkv_ref,
                     wq_ref, wsw_ref, qcat_ref, ckv_o, ckvb_o, kr_o, krp_o):
    cq = _rms(cq_ref[...], gq_ref[...]).astype(BF16)
    ckv = _rms(ckv_ref[...], gkv_ref[...])
    ckv_o[...] = ckv
    ckvb_o[...] = ckv.astype(BF16)
    cos = cos_ref[...]
    sin = sin_ref[...]
    r = a_ref[...] * cos + b_ref[...] * sin
    kr_o[...] = r[:, :ROPE_DIM]
    krp_o[...] = r.astype(BF16)
    for h in range(N_HEADS):
        qh = jnp.dot(cq, wq_ref[:, h * Q_CAT:(h + 1) * Q_CAT], preferred_element_type=F32)
        qs = jnp.dot(cq, wsw_ref[:, h * LANES:(h + 1) * LANES], preferred_element_type=F32)
        rp = qh[:, LANES:] * cos + qs * sin
        qcat = jnp.concatenate([qh[:, :LANES], rp], axis=1) * (MLA_SCALE * LOG2E)
        qcat_ref[h] = qcat.astype(BF16)


def _mla_prep(zmisc, cos_t, sin_t, gq, gkv, wq_cat, wq_sw, tm):
    m = zmisc.shape[0]
    row = lambda w, j: pl.BlockSpec((tm, w), lambda i, j=j: (i, j))
    const = lambda a: pl.BlockSpec(a.shape, lambda i: (0,) * a.ndim)
    gq2, gkv2 = gq.reshape(1, Q_LORA), gkv.reshape(1, KV_LORA)
    return pl.pallas_call(
        _mla_prep_kernel,
        grid=(m // tm,),
        in_specs=[row(Q_LORA, MISC_CQ // Q_LORA), row(KV_LORA, MISC_CKV // KV_LORA),
                  row(LANES, MISC_KR // LANES), row(LANES, MISC_KRSW // LANES),
                  row(LANES, 0), row(LANES, 0), const(gq2), const(gkv2),
                  const(wq_cat), const(wq_sw)],
        out_specs=[pl.BlockSpec((N_HEADS, tm, Q_CAT), lambda i: (0, i, 0)),
                   row(KV_LORA, 0), row(KV_LORA, 0), row(ROPE_DIM, 0), row(LANES, 0)],
        out_shape=[jax.ShapeDtypeStruct((N_HEADS, m, Q_CAT), BF16),
                   jax.ShapeDtypeStruct((m, KV_LORA), F32),
                   jax.ShapeDtypeStruct((m, KV_LORA), BF16),
                   jax.ShapeDtypeStruct((m, ROPE_DIM), F32),
                   jax.ShapeDtypeStruct((m, LANES), BF16)],
        compiler_params=_params(("parallel",)),
        name="mla_prep",
    )(zmisc, zmisc, zmisc, zmisc, cos_t, sin_t, gq2, gkv2, wq_cat, wq_sw)


SCORE_TILE = 256


def _indexer_kernel(q_ref, wb_ref, kT_ref, mask_ref, key_sc, w_sc, q_sc, *, tq, ts, cw, topk):
    i = pl.program_id(0)
    s_total = kT_ref.shape[2]
    t0 = i * tq
    front = t0 + tq
    nblk = (front + ts - 1) // ts
    nch = nblk * (ts // cw)

    w = wb_ref[...]
    off = MISC_IXW - MISC_KRSW
    for h in range(IDX_HEADS):
        w_sc[h] = jnp.broadcast_to(w[:, off + h:off + h + 1] * (IDX_HEADS ** -0.5), (tq, LANES))
    qb = q_ref[...].astype(BF16)
    npair = IDX_HEADS // 2
    for j in range(npair):
        q_sc[j * tq:(j + 1) * tq, :] = qb[:, j * LANES:(j + 1) * LANES]

    def score_block(kb, rmax):
        c0 = pl.multiple_of(kb * ts, ts)
        for c in range(ts // SCORE_TILE):
            cc = pl.multiple_of(c0 + c * SCORE_TILE, SCORE_TILE)
            w_both = jnp.concatenate([kT_ref[0, :, pl.ds(cc, SCORE_TILE)],
                                      kT_ref[1, :, pl.ds(cc, SCORE_TILE)]], axis=1)
            d_both = jnp.dot(q_sc[...], w_both, preferred_element_type=F32)
            acc = jnp.zeros((tq, SCORE_TILE), F32)
            for h in range(IDX_HEADS):
                d = d_both[(h // 2) * tq:(h // 2 + 1) * tq, (h % 2) * SCORE_TILE:(h % 2 + 1) * SCORE_TILE]
                acc = acc + jnp.maximum(d, 0.0) * _lane_tile(w_sc[h], SCORE_TILE // LANES)
            rows = t0 + lax.broadcasted_iota(I32, (tq, SCORE_TILE), 0)
            cols = cc + lax.broadcasted_iota(I32, (tq, SCORE_TILE), 1)
            adm = (cols >> CHUNK_SHIFT) <= (rows >> CHUNK_SHIFT)
            sc = jnp.where(adm, acc, NEG_INF)
            bits = lax.bitcast_convert_type(sc, I32)
            key_sc[:, pl.ds(cc, SCORE_TILE)] = bits ^ ((bits >> 31) & 0x7FFFFFFF)
            for j in range(SCORE_TILE // LANES):
                rmax = jnp.maximum(rmax, sc[:, j * LANES:(j + 1) * LANES])
        return rmax

    rmax = lax.fori_loop(0, nblk, score_block, jnp.full((tq, LANES), NEG_INF, F32))

    kf = float(topk)

    def count_ge(cand):
        def body(c, cnt):
            c0 = pl.multiple_of(c * cw, cw)
            k = key_sc[:, pl.ds(c0, cw)]
            for j in range(cw // LANES):
                cnt = cnt + jnp.where(k[:, j * LANES:(j + 1) * LANES] >= cand, 1.0, 0.0)
            return cnt
        cnt = lax.fori_loop(0, nch, body, jnp.zeros((tq, LANES), F32))
        return jnp.broadcast_to(jnp.sum(cnt, axis=1, keepdims=True), (tq, LANES))

    rows1 = t0 + lax.broadcasted_iota(I32, (tq, LANES), 0)
    few = (((rows1 >> CHUNK_SHIFT) + 1) << CHUNK_SHIFT) < topk
    row_max = jnp.broadcast_to(jnp.max(rmax, axis=1, keepdims=True), (tq, LANES))
    lo = jnp.full((tq, LANES), KEY_MASKED, I32)
    hi = _sort_key(row_max) + 1
    cnt_lo = jnp.broadcast_to((nch * cw).astype(F32), (tq, LANES))
    cnt_hi = jnp.zeros((tq, LANES), F32)
    log_k = math.log(topk)

    def key_value(k):
        return lax.bitcast_convert_type(k ^ ((k >> 31) & 0x7FFFFFFF), F32)

    def done_of(lo, hi, cnt_lo):
        return few | (cnt_lo == kf) | (hi - 1 <= lo)

    def search_cond(st):
        return st[-1] > 0.5

    scale = jnp.maximum(jnp.abs(row_max), 1e-30)
    ones_v = jnp.ones((tq, LANES), F32)

    def search_step(st):
        lo, hi, cnt_lo, cnt_hi, w_lo, w_hi, last, step, _ = st
        done = done_of(lo, hi, cnt_lo)
        mid = (lo >> 1) + (hi >> 1) + (lo & hi & 1)
        lo_v, hi_v = key_value(lo), key_value(hi)
        f_lo = (jnp.log(cnt_lo) - log_k) * w_lo
        f_hi = (jnp.log(jnp.maximum(cnt_hi, 0.5)) - log_k) * w_hi
        guess = _sort_key(lo_v + f_lo / (f_lo - f_hi) * (hi_v - lo_v))
        grow = jnp.exp2(jnp.broadcast_to(jnp.minimum(step - 1, 100).astype(F32), (tq, LANES)))
        guess = jnp.where(lo <= KEY_HALF, _sort_key(-scale * grow), guess)
        first = jnp.zeros((tq, LANES), I32)
        cand = jnp.where(step == 0, first, jnp.where(step % 4 == 3, mid, guess))
        cand = jnp.minimum(jnp.maximum(cand, lo + 1), hi - 1)
        cnt = count_ge(cand)
        up = (cnt >= kf) & jnp.logical_not(done)
        down = (cnt < kf) & jnp.logical_not(done)
        lo = jnp.where(up, cand, lo)
        cnt_lo = jnp.where(up, cnt, cnt_lo)
        hi = jnp.where(down, cand, hi)
        cnt_hi = jnp.where(down, cnt, cnt_hi)
        w_hi = jnp.where(up, jnp.where(last > 0.5, 0.5 * w_hi, ones_v), jnp.where(down, ones_v, w_hi))
        w_lo = jnp.where(down, jnp.where(last < -0.5, 0.5 * w_lo, ones_v), jnp.where(up, ones_v, w_lo))
        last = jnp.where(up, 1.0, jnp.where(down, -1.0, last))
        active = jnp.max(jnp.where(done_of(lo, hi, cnt_lo), 0.0, 1.0))
        return lo, hi, cnt_lo, cnt_hi, w_lo, w_hi, last, step + 1, active

    active0 = jnp.max(jnp.where(done_of(lo, hi, cnt_lo), 0.0, 1.0))
    lo, hi, cnt_lo, cnt_hi = lax.while_loop(
        search_cond, search_step,
        (lo, hi, cnt_lo, cnt_hi, ones_v, ones_v, jnp.zeros((tq, LANES), F32), jnp.int32(0), active0))[:4]
    ans = jnp.where(few, INT_MIN, lo)
    thr = _lane_tile(jnp.maximum(ans, KEY_HALF + 1), cw // LANES)
    tied = jnp.logical_not(few) & (cnt_lo > kf) & (lo > KEY_HALF)
    any_tied = jnp.max(jnp.where(tied, 1.0, 0.0))

    def write_mask(c, carry):
        c0 = pl.multiple_of(c * cw, cw)
        sel = key_sc[:, pl.ds(c0, cw)] >= thr
        mask_ref[:, pl.ds(c0, cw)] = jnp.where(sel, 0.0, NEG_INF).astype(BF16)
        return carry

    @pl.when(any_tied < 0.5)
    def _plain():
        lax.fori_loop(0, nch, write_mask, 0)

    @pl.when(any_tied >= 0.5)
    def _with_ties():
        need = kf - count_ge(lo + 1)
        lo_t = _lane_tile(lo, cw // LANES)
        tied_t = _lane_tile(jnp.where(tied, 1.0, 0.0), cw // LANES) > 0.5
        need_t = _lane_tile(need, cw // LANES)
        jj = lax.broadcasted_iota(I32, (cw, cw), 0)
        ii = lax.broadcasted_iota(I32, (cw, cw), 1)
        upper = jnp.where(jj <= ii, 1.0, 0.0).astype(BF16)

        def write_tied(c, seen):
            c0 = pl.multiple_of(c * cw, cw)
            k = key_sc[:, pl.ds(c0, cw)]
            eq = jnp.where(k == lo_t, 1.0, 0.0)
            incl = jnp.dot(eq.astype(BF16), upper, preferred_element_type=F32)
            rank = _lane_tile(seen, cw // LANES) + incl - eq
            take = (k > lo_t) | ((eq > 0.5) & (rank < need_t))
            sel = (tied_t & take) | (jnp.logical_not(tied_t) & (k >= thr))
            mask_ref[:, pl.ds(c0, cw)] = jnp.where(sel, 0.0, NEG_INF).astype(BF16)
            return seen + jnp.broadcast_to(incl[:, cw - 1:cw], (tq, LANES))

        lax.fori_loop(0, nch, write_tied, jnp.zeros((tq, LANES), F32))

    def zero_mask(c, carry):
        c0 = pl.multiple_of(c * cw, cw)
        mask_ref[:, pl.ds(c0, cw)] = jnp.full((tq, cw), NEG_INF, BF16)
        return carry

    lax.fori_loop(nch, s_total // cw, zero_mask, 0)


def _indexer(zmisc, ixk_t, topk, tq=128, ts=1024, cw=1024):
    s = ixk_t.shape[1]
    assert ts % cw == 0 and s % ts == 0 and ts % tq == 0
    zero = jnp.zeros_like(ixk_t)
    ixk_t = jnp.stack([jnp.concatenate([ixk_t, zero], axis=0), jnp.concatenate([zero, ixk_t], axis=0)])
    return pl.pallas_call(
        functools.partial(_indexer_kernel, tq=tq, ts=ts, cw=cw, topk=topk),
        grid=(s // tq,),
        in_specs=[pl.BlockSpec((tq, IDX_HEADS * IDX_DIM), lambda i: (i, 0)),
                  pl.BlockSpec((tq, LANES), lambda i: (i, MISC_KRSW // LANES)),
                  pl.BlockSpec((2, 2 * IDX_DIM, s), lambda i: (0, 0, 0))],
        out_specs=pl.BlockSpec((tq, s), lambda i: (i, 0)),
        out_shape=jax.ShapeDtypeStruct((s, s), BF16),
        scratch_shapes=[pltpu.VMEM((tq, s), I32),
                        pltpu.VMEM((IDX_HEADS, tq, LANES), F32),
                        pltpu.VMEM((IDX_HEADS // 2 * tq, 2 * IDX_DIM), BF16)],
        compiler_params=_params(("parallel",)),
        name="indexer",
    )(zmisc, zmisc, ixk_t)


_NT = (((1,), (1,)), ((), ()))
FLASH_AHEAD = 2
FLASH_GROUP = 16


def _flash_kernel(qi_ref, ki_ref, *refs, kind, tb):
    if kind == "dsa":
        q_ref, k_ref, v_ref, mask_ref, b_ref, o_ref, m_sc, l_sc, acc_sc, mb_sc = refs
    else:
        q_ref, k_ref, kr_ref, v_ref, o_ref, m_sc, l_sc, acc_sc, mb_sc = refs
    p = pl.program_id(0)
    qi = qi_ref[p]
    ki = ki_ref[p]
    nsub = tb // LANES

    @pl.when(ki == 0)
    def _init():
        m_sc[...] = jnp.full(m_sc.shape, NEG_INF, F32)
        l_sc[...] = jnp.zeros(l_sc.shape, F32)
        acc_sc[...] = jnp.zeros(acc_sc.shape, F32)

    ones = jnp.ones((tb, LANES), BF16)

    def update(h, s):
        m_prev = m_sc[h]
        m_new = jnp.maximum(m_prev, jnp.max(s, axis=1, keepdims=True))
        alpha = jnp.exp2(m_prev - m_new)
        pr = jnp.exp2(s - _lane_tile(m_new, nsub)).astype(BF16)
        pv = jnp.dot(pr, jnp.concatenate([v_ref[h], ones], axis=1), preferred_element_type=F32)
        acc_sc[h] = alpha * acc_sc[h] + pv[:, :LANES]
        l_sc[h] = alpha * l_sc[h] + pv[:, LANES:]
        m_sc[h] = m_new

    def run_heads(logits):
        def group(g, carry):
            base = g * FLASH_GROUP
            ahead = [logits(base + j) for j in range(min(FLASH_AHEAD, FLASH_GROUP))]
            for j in range(FLASH_GROUP):
                s = ahead.pop(0)
                if j + FLASH_AHEAD < FLASH_GROUP:
                    ahead.append(logits(base + j + FLASH_AHEAD))
                update(base + j, s)
            return carry

        lax.fori_loop(0, N_HEADS // FLASH_GROUP, group, 0)

    if kind == "dsa":
        mb_sc[...] = mask_ref[...].astype(F32)

        def logits_far(h):
            return lax.dot_general(q_ref[h], k_ref[h], _NT, preferred_element_type=F32) + mb_sc[...]

        on_diag = ki == qi
        i_main = jnp.where(on_diag, 0, 2)
        i_sub = jnp.where(on_diag, 1, 2)
        i_corner = jnp.where(on_diag, 2, 1)

        def logits_near(h):
            s = logits_far(h)
            rows = []
            for a in range(nsub):
                blocks = []
                for b in range(nsub):
                    blk = s[a * LANES:(a + 1) * LANES, b * LANES:(b + 1) * LANES]
                    if b == a:
                        blk = blk + b_ref[i_main, h]
                    elif b == a - 1:
                        blk = blk + b_ref[i_sub, h]
                    elif a == 0 and b == nsub - 1:
                        blk = blk + b_ref[i_corner, h]
                    blocks.append(blk)
                rows.append(jnp.concatenate(blocks, axis=1))
            return jnp.concatenate(rows, axis=0)

        @pl.when(ki < qi - 1)
        def _far():
            run_heads(logits_far)

        @pl.when(ki >= qi - 1)
        def _near():
            run_heads(logits_near)
    else:
        @pl.when(p == 0)
        def _masks():
            rows = lax.broadcasted_iota(I32, (tb, tb), 0)
            cols = lax.broadcasted_iota(I32, (tb, tb), 1)
            vis = (cols >> CHUNK_SHIFT) <= (rows >> CHUNK_SHIFT)
            mb_sc[...] = jnp.where(vis, 0.0, NEG_INF)

        def qk(h):
            k = jnp.concatenate([k_ref[h], kr_ref[...]], axis=1)
            return lax.dot_general(q_ref[h], k, _NT, preferred_element_type=F32)

        @pl.when(ki < qi)
        def _off():
            run_heads(qk)

        @pl.when(ki == qi)
        def _diag():
            run_heads(lambda h: qk(h) + mb_sc[...])

    @pl.when(ki == qi)
    def _finish():
        for h in range(N_HEADS):
            o_ref[:, h * LANES:(h + 1) * LANES] = acc_sc[h] / l_sc[h]


def _flash(kind, q, k, v, extra, seq, tb=512):
    nq = seq // tb
    pairs = [(a, b) for a in range(nq) for b in range(a + 1)]
    qi = jnp.asarray([a for a, _ in pairs], I32)
    ki = jnp.asarray([b for _, b in pairs], I32)
    dq = q.shape[2]
    qspec = pl.BlockSpec((N_HEADS, tb, dq), lambda p, qi, ki: (0, qi[p], 0))
    scratch = [pltpu.VMEM((N_HEADS, tb, LANES), F32), pltpu.VMEM((N_HEADS, tb, LANES), F32),
               pltpu.VMEM((N_HEADS, tb, LANES), F32)]
    if kind == "dsa":
        mask, btiles = extra
        in_specs = [qspec,
                    pl.BlockSpec((N_HEADS, tb, LANES), lambda p, qi, ki: (0, ki[p], 0)),
                    pl.BlockSpec((N_HEADS, tb, LANES), lambda p, qi, ki: (0, ki[p], 0)),
                    pl.BlockSpec((tb, tb), lambda p, qi, ki: (qi[p], ki[p])),
                    pl.BlockSpec(btiles.shape, lambda p, qi, ki: (0, 0, 0, 0))]
        args = (q, k, v, mask, btiles)
        scratch = scratch + [pltpu.VMEM((tb, tb), F32)]
    else:
        (krp,) = extra
        in_specs = [qspec,
                    pl.BlockSpec((N_HEADS, tb, LANES), lambda p, qi, ki: (0, ki[p], 0)),
                    pl.BlockSpec((tb, LANES), lambda p, qi, ki: (ki[p], 0)),
                    pl.BlockSpec((N_HEADS, tb, LANES), lambda p, qi, ki: (1, ki[p], 0))]
        args = (q, k, krp, v)
        scratch = scratch + [pltpu.VMEM((tb, tb), F32)]
    return pl.pallas_call(
        functools.partial(_flash_kernel, kind=kind, tb=tb),
        grid_spec=pltpu.PrefetchScalarGridSpec(
            num_scalar_prefetch=2,
            grid=(len(pairs),),
            in_specs=in_specs,
            out_specs=pl.BlockSpec((tb, N_HEADS * LANES), lambda p, qi, ki: (qi[p], 0)),
            scratch_shapes=scratch),
        out_shape=jax.ShapeDtypeStruct((seq, N_HEADS * LANES), F32),
        compiler_params=_params(("arbitrary",)),
        name="flash_" + kind,
    )(qi, ki, *args)


def _sample_kernel(ixq_ref, wb_ref, ikT_ref, aq_ref, kc_ref, vc_ref, kn_ref, vn_ref, bias_ref, qc_ref,
                   kvb_ref, krp_ref, oa_ref, ob_ref, *, n_new, s_valid, past, topk):
    sp = ikT_ref.shape[2]
    qpos = past + lax.broadcasted_iota(I32, (n_new, sp), 0)
    kpos = lax.broadcasted_iota(I32, (n_new, sp), 1)
    adm = (kpos < s_valid) & ((kpos >> CHUNK_SHIFT) <= (qpos >> CHUNK_SHIFT))

    qb = ixq_ref[...].astype(BF16)
    w = wb_ref[...]
    kT = ikT_ref[0]
    off = MISC_IXW - MISC_KRSW
    score = jnp.zeros((n_new, sp), F32)
    for h in range(IDX_HEADS):
        d = jnp.dot(qb[:, h * IDX_DIM:(h + 1) * IDX_DIM], kT, preferred_element_type=F32)
        score = score + jnp.maximum(d, 0.0) * (w[:, off + h:off + h + 1] * (IDX_HEADS ** -0.5))
    key = jnp.where(adm, _sort_key(score), KEY_MASKED)

    kf = float(topk)

    def count_ge(cand):
        return jnp.sum(jnp.where(key >= cand, 1.0, 0.0), axis=1, keepdims=True)

    ans = jnp.where(count_ge(jnp.zeros((n_new, 1), I32)) >= kf, 0, INT_MIN).astype(I32)

    def bit_step(b, ans):
        cand = ans + jnp.left_shift(jnp.int32(1), 30 - b)
        return jnp.where(count_ge(cand) >= kf, cand, ans)

    ans = lax.fori_loop(0, 31, bit_step, ans)
    need = kf - count_ge(ans + 1)
    eq = jnp.where(key == ans, 1.0, 0.0)
    jj = lax.broadcasted_iota(I32, (sp, sp), 0)
    ii = lax.broadcasted_iota(I32, (sp, sp), 1)
    before = jnp.where(jj < ii, 1.0, 0.0).astype(BF16)
    rank = jnp.dot(eq.astype(BF16), before, preferred_element_type=F32)
    sel = ((key > ans) | ((eq > 0.5) & (rank < need))) & (key > KEY_HALF)

    def attend(s, vmat):
        m = jnp.max(s, axis=1, keepdims=True)
        pr = jnp.exp2(s - m)
        l = jnp.sum(pr, axis=1, keepdims=True)
        return jnp.dot(pr.astype(BF16), vmat, preferred_element_type=F32) / l

    row_pad = jnp.zeros((sp - past - n_new, LANES), BF16)
    for h in range(N_HEADS):
        hs = slice(h * LANES, (h + 1) * LANES)
        kn = jnp.concatenate([kn_ref[h], row_pad], axis=0)
        vn = jnp.concatenate([vn_ref[h], row_pad], axis=0)
        q = aq_ref[h]
        s = jnp.concatenate([lax.dot_general(q, kc_ref[0, h], _NT, preferred_element_type=F32),
                             lax.dot_general(q, kn, _NT, preferred_element_type=F32)], axis=1)
        s = jnp.where(sel, s + bias_ref[h], NEG_INF)
        m = jnp.max(s, axis=1, keepdims=True)
        pr = jnp.exp2(s - m)
        l = jnp.sum(pr, axis=1, keepdims=True)
        pr = pr.astype(BF16)
        pv = (jnp.dot(pr[:, :past], vc_ref[0, h], preferred_element_type=F32)
              + jnp.dot(pr[:, past:], vn, preferred_element_type=F32))
        oa_ref[:, hs] = pv / l

    krp = krp_ref[0]
    hd = N_HEADS * LANES
    for h in range(N_HEADS):
        hs = slice(h * LANES, (h + 1) * LANES)
        k = jnp.concatenate([kvb_ref[0, :, hs], krp], axis=1)
        s = lax.dot_general(qc_ref[h], k, _NT, preferred_element_type=F32)
        s = jnp.where(adm, s, NEG_INF)
        ob_ref[:, hs] = attend(s, kvb_ref[0, :, hd + h * LANES:hd + (h + 1) * LANES])


def _sample_attn(zmisc, ixk_t, aq_hm, kc, vc, kn_hm, vn_hm, bias, qcat, kvb, krp, n_new, s_valid, past,
                 topk):
    nb = kc.shape[0]
    hd = N_HEADS * LANES
    assert past % LANES == 0 and n_new <= LANES and ixk_t.shape[2] == past + LANES
    new_rows = pl.BlockSpec((N_HEADS, n_new, LANES), lambda b: (0, b, 0))
    per_stream = lambda a: pl.BlockSpec((1,) + a.shape[1:], lambda b: (b,) + (0,) * (a.ndim - 1))
    return pl.pallas_call(
        functools.partial(_sample_kernel, n_new=n_new, s_valid=s_valid, past=past, topk=topk),
        grid=(nb,),
        in_specs=[pl.BlockSpec((n_new, IDX_HEADS * IDX_DIM), lambda b: (b, 0)),
                  pl.BlockSpec((n_new, LANES), lambda b: (b, MISC_KRSW // LANES)),
                  per_stream(ixk_t),
                  new_rows, per_stream(kc), per_stream(vc), new_rows, new_rows,
                  pl.BlockSpec(bias.shape, lambda b: (0, 0, 0)),
                  pl.BlockSpec((N_HEADS, n_new, Q_CAT), lambda b: (0, b, 0)),
                  per_stream(kvb), per_stream(krp)],
        out_specs=[pl.BlockSpec((n_new, hd), lambda b: (b, 0)),
                   pl.BlockSpec((n_new, hd), lambda b: (b, 0))],
        out_shape=[jax.ShapeDtypeStruct((nb * n_new, hd), F32),
                   jax.ShapeDtypeStruct((nb * n_new, hd), F32)],
        compiler_params=_params(("parallel",)),
        name="sample_attn",
    )(zmisc, zmisc, ixk_t, aq_hm, kc, vc, kn_hm, vn_hm, bias, qcat, kvb, krp)


def _outproj_kernel(x_ref, ga_ref, gb_ref, oa_ref, ob_ref, w_ref, o_ref):
    mix = ga_ref[...] * oa_ref[...] + gb_ref[...] * ob_ref[...]
    o_ref[...] = x_ref[...] + jnp.dot(mix.astype(BF16), w_ref[...], preferred_element_type=F32)


def _outproj(x, ga, gb, oa, ob, w, tm):
    m, d = x.shape
    row = pl.BlockSpec((tm, d), lambda i: (i, 0))
    return pl.pallas_call(
        _outproj_kernel,
        grid=(m // tm,),
        in_specs=[row, row, row, row, row, pl.BlockSpec(w.shape, lambda i: (0, 0))],
        out_specs=row,
        out_shape=jax.ShapeDtypeStruct((m, d), F32),
        compiler_params=_params(("parallel",)),
        name="outproj",
    )(x, ga, gb, oa, ob, w)


def _ffn_kernel(x_ref, g_ref, wu_ref, wd_ref, gf_ref, o_ref, h_sc, acc_sc):
    f = pl.program_id(1)

    @pl.when(f == 0)
    def _start():
        h_sc[...] = _rms(x_ref[...], g_ref[...]).astype(BF16)
        acc_sc[...] = jnp.zeros(acc_sc.shape, F32)

    u = jnp.dot(h_sc[...], wu_ref[...], preferred_element_type=F32)
    u = jnp.square(jnp.maximum(u, 0.0)).astype(BF16)
    acc_sc[...] += jnp.dot(u, wd_ref[...], preferred_element_type=F32)

    @pl.when(f == pl.num_programs(1) - 1)
    def _end():
        o_ref[...] = _rms(x_ref[...] + acc_sc[...], gf_ref[...])


def _ffn(x, g, wu, wd, gf, tm, tf=1024):
    m, d = x.shape
    dff = wu.shape[1]
    return pl.pallas_call(
        _ffn_kernel,
        grid=(m // tm, dff // tf),
        in_specs=[pl.BlockSpec((tm, d), lambda i, f: (i, 0)),
                  pl.BlockSpec((1, d), lambda i, f: (0, 0)),
                  pl.BlockSpec((d, tf), lambda i, f: (0, f)),
                  pl.BlockSpec((tf, d), lambda i, f: (f, 0)),
                  pl.BlockSpec((1, d), lambda i, f: (0, 0))],
        out_specs=pl.BlockSpec((tm, d), lambda i, f: (i, 0)),
        out_shape=jax.ShapeDtypeStruct((m, d), F32),
        scratch_shapes=[pltpu.VMEM((tm, d), BF16), pltpu.VMEM((tm, d), F32)],
        compiler_params=_params(("parallel", "arbitrary")),
        name="ffn",
    )(x, g.reshape(1, d), wu, wd, gf.reshape(1, d))


def _t5_bucket(rel):
    nb = REL_BUCKETS // 2
    ret = (rel > 0).astype(jnp.int32) * nb
    n = jnp.abs(rel)
    max_exact = nb // 2
    large = max_exact + (jnp.log(jnp.maximum(n, 1).astype(jnp.float32) / max_exact)
                         / math.log(REL_MAX_DIST / max_exact) * (nb - max_exact)).astype(jnp.int32)
    large = jnp.minimum(large, nb - 1)
    return ret + jnp.where(n < max_exact, n, large)


def _rel_bias(table, rel):
    onehot = jax.nn.one_hot(_t5_bucket(rel), REL_BUCKETS, dtype=F32)
    return jnp.einsum("...b,bh->...h", onehot, table.astype(F32), precision=lax.Precision.HIGHEST)


def _rope_tables(pos):
    half = ROPE_DIM // 2
    inv_freq = jnp.power(ROPE_THETA, -jnp.arange(half, dtype=jnp.float32) / half)
    ang = pos.astype(jnp.float32)[:, None] * inv_freq[None, :]
    cos, sin = jnp.cos(ang), jnp.sin(ang)
    pad = jnp.zeros((pos.shape[0], LANES - ROPE_DIM), F32)
    return (jnp.concatenate([cos, cos, pad], axis=1), jnp.concatenate([-sin, sin, pad], axis=1))


def _swap_halves(w):
    half = ROPE_DIM // 2
    return jnp.concatenate([w[..., half:], w[..., :half]], axis=-1)


def _prep_weights(w_in, w_uq, w_uk, w_uv, w_out, w_ff_up, w_ff_down):
    d = w_in.shape[0]
    hd = N_HEADS * HEAD_DIM
    sizes = (hd, hd, hd, IDX_HEADS * IDX_DIM, IDX_DIM, IDX_HEADS, Q_LORA, KV_LORA, ROPE_DIM, d, d)
    cols, off = [], 0
    for n in sizes:
        cols.append(w_in[:, off:off + n])
        off += n
    wq, wk, wv, wixq, wixk, wixw, wcq, wckv, wkr, wga, wgb = cols
    pad = jnp.zeros((d, MISC_COLS - MISC_IXW - IDX_HEADS), w_in.dtype)
    wmisc = jnp.concatenate([wixq, wcq, wckv, wkr, wixk, _swap_halves(wkr), wixw, pad], axis=1)
    nope, rope = w_uq[..., :QK_NOPE_DIM], w_uq[..., QK_NOPE_DIM:]
    zpad = jnp.zeros(rope.shape, w_uq.dtype)
    wq_cat = jnp.concatenate([nope, rope, zpad], axis=-1).reshape(Q_LORA, N_HEADS * Q_CAT)
    wq_sw = jnp.concatenate([_swap_halves(rope), zpad], axis=-1).reshape(Q_LORA, N_HEADS * LANES)
    wkv = jnp.concatenate([w_uk.reshape(KV_LORA, hd), w_uv.reshape(KV_LORA, hd)], axis=1)
    c = lambda a: a.astype(BF16)
    return dict(q=c(wq), k=c(wk), v=c(wv), misc=c(wmisc), ga=c(wga), gb=c(wgb), wq_cat=c(wq_cat),
                wq_sw=c(wq_sw), wkv=c(wkv), out=c(w_out), up=c(w_ff_up), down=c(w_ff_down))


def _front(x, pos, w, g_mix, g_q, g_kv, tm):
    h = _norm_cast(x, g_mix, tm)
    aq_hm = _mm(h, w["q"], "heads", tm, head_scale=A_SCALE * LOG2E)
    ak, ak_hm = _mm(h, w["k"], "f32+heads", tm)
    av, av_hm = _mm(h, w["v"], "f32+heads", tm)
    zmisc = _mm(h, w["misc"], "f32", tm)
    sga = _mm(h, w["ga"], "sigmoid", tm)
    sgb = _mm(h, w["gb"], "sigmoid", tm)
    cos_t, sin_t = _rope_tables(pos)
    qcat, ckv, ckv_b, kr, krp = _mla_prep(zmisc, cos_t, sin_t, g_q, g_kv, w["wq_cat"], w["wq_sw"], tm)
    return dict(aq_hm=aq_hm, ak=ak, ak_hm=ak_hm, av=av, av_hm=av_hm, zmisc=zmisc, sga=sga, sgb=sgb,
                qcat=qcat, ckv=ckv, ckv_b=ckv_b, kr=kr, krp=krp)


def _back(x, f, oa, ob, w, g_ffn, g_final, tm_out, tm_ffn):
    x1 = _outproj(x, f["sga"], f["sgb"], oa, ob, w["out"], tm_out)
    return _ffn(x1, g_ffn, w["up"], w["down"], g_final, tm_ffn)


def kernel(x_prompt, x_sample, cache_a_k, cache_a_v, cache_a_idx_k, cache_b_ckv, cache_b_krope,
           rel_bias_table, norm_mix_g, w_in, q_lora_g, w_uq, kv_lora_g, w_uk, w_uv, w_out,
           norm_ffn_g, w_ff_up, w_ff_down, final_norm_g):
    assert w_in.shape[0] == 1, "single-layer trunk"
    _, seq, d = x_prompt.shape
    nb, n_new, _ = x_sample.shape
    past = cache_a_k.shape[2]
    s_valid = past + n_new
    sp = -(-s_valid // LANES) * LANES
    topk_p = min(TOPK_MAX, seq // 4)
    topk_s = min(TOPK_MAX, s_valid // 4)
    hd = N_HEADS * HEAD_DIM

    w = _prep_weights(w_in[0], w_uq[0], w_uk[0], w_uv[0], w_out[0], w_ff_up[0], w_ff_down[0])

    xp = x_prompt.reshape(seq, d)
    fp = _front(xp, jnp.arange(seq, dtype=jnp.int32), w, norm_mix_g[0], q_lora_g[0], kv_lora_g[0], 512)
    ixk_p = fp["zmisc"][:, MISC_IXK:MISC_IXK + IDX_DIM]
    mask = _indexer(fp["zmisc"], ixk_p.astype(BF16).T, topk_p)
    ii = jnp.arange(LANES, dtype=jnp.int32)
    rel0 = ii[None, :] - ii[:, None]
    far = _rel_bias(rel_bias_table, jnp.full((1, 1), -REL_MAX_DIST, jnp.int32))
    btiles = jnp.stack([_rel_bias(rel_bias_table, rel0) - far,
                        _rel_bias(rel_bias_table, rel0 - LANES) - far,
                        jnp.zeros((LANES, LANES, N_HEADS), F32)])
    btiles = jnp.transpose(btiles, (0, 3, 1, 2)).astype(F32) * LOG2E
    oa_p = _flash("dsa", fp["aq_hm"], fp["ak_hm"], fp["av_hm"], (mask, btiles), seq)
    kvb_p = _mm(fp["ckv_b"], w["wkv"], "heads", 512)
    ob_p = _flash("mla", fp["qcat"], kvb_p, kvb_p, (fp["krp"],), seq)
    y_p = _back(xp, fp, oa_p, ob_p, w, norm_ffn_g[0], final_norm_g, 256, 512)

    m_s = nb * n_new
    xs = x_sample.reshape(m_s, d)
    pos_s = jnp.tile(past + jnp.arange(n_new, dtype=jnp.int32), nb)
    fs = _front(xs, pos_s, w, norm_mix_g[0], q_lora_g[0], kv_lora_g[0], m_s)
    ixk_s = fs["zmisc"][:, MISC_IXK:MISC_IXK + IDX_DIM]

    def with_cache(cache, new, width):
        a = jnp.concatenate([cache.reshape(nb, past, width).astype(BF16),
                             new.reshape(nb, n_new, width).astype(BF16)], axis=1)
        return jnp.pad(a, ((0, 0), (0, sp - s_valid), (0, 0)))

    kc = jnp.transpose(cache_a_k[0].astype(BF16), (0, 2, 1, 3))
    vc = jnp.transpose(cache_a_v[0].astype(BF16), (0, 2, 1, 3))
    ixk_all = jnp.swapaxes(with_cache(cache_a_idx_k[0], ixk_s, IDX_DIM), 1, 2)
    ckv_all = with_cache(cache_b_ckv[0], fs["ckv"], KV_LORA)
    krope_pad = jnp.pad(cache_b_krope[0], ((0, 0), (0, 0), (0, LANES - ROPE_DIM)))
    krp_all = with_cache(krope_pad, fs["krp"], LANES)
    kvb_s = _mm(ckv_all.reshape(nb * sp, KV_LORA), w["wkv"], "bf16", sp).reshape(nb, sp, 2 * hd)
    qpos = past + jnp.arange(n_new, dtype=jnp.int32)
    kpos = jnp.arange(sp, dtype=jnp.int32)
    bias_s = jnp.transpose(_rel_bias(rel_bias_table, kpos[None, :] - qpos[:, None]), (2, 0, 1))
    oa_s, ob_s = _sample_attn(fs["zmisc"], ixk_all, fs["aq_hm"], kc, vc, fs["ak_hm"], fs["av_hm"],
                              bias_s.astype(F32) * LOG2E,
                              fs["qcat"], kvb_s, krp_all, n_new, s_valid, past, topk_s)
    y_s = _back(xs, fs, oa_s, ob_s, w, norm_ffn_g[0], final_norm_g, m_s, m_s)

    st = lambda a, b, *tail: a.reshape((1, b, -1) + tail)
    return (y_p.reshape(1, seq, d), y_s.reshape(nb, n_new, d),
            st(fp["ak"], 1, N_HEADS, HEAD_DIM), st(fp["av"], 1, N_HEADS, HEAD_DIM),
            st(ixk_p, 1, IDX_DIM), st(fp["ckv"], 1, KV_LORA), st(fp["kr"], 1, ROPE_DIM),
            st(fs["ak"], nb, N_HEADS, HEAD_DIM), st(fs["av"], nb, N_HEADS, HEAD_DIM),
            st(ixk_s, nb, IDX_DIM), st(fs["ckv"], nb, KV_LORA), st(fs["kr"], nb, ROPE_DIM))
```
